```python
import jax, jax.numpy as jnp
from jax import lax
import numpy as np

D_MODEL = 2048
BATCH = 4
SEQ = 4096
DEPTH = 2

N_MIXERS = 2
N_CONV_LAYERS = (DEPTH + 1) // 2
N_ATTN_LAYERS = DEPTH // 2
N_META = 16
CONV_WIDTH = 3
N_HEADS = 16
HEAD_DIM = D_MODEL // N_HEADS
Q_BLOCK = 128
N_GROUPS = 4
EXPERTS_PER_GROUP = 8
N_EXPERTS = N_GROUPS * EXPERTS_PER_GROUP
TOP_K = 2
D_EXPERT = D_MODEL // 4
EPS = 1e-6

kernel_name = "hybrid_shortconv_stickbreaking_hmoe"


def rms_norm(x, g):
    x32 = x.astype(jnp.float32)
    y = x32 * lax.rsqrt(jnp.mean(x32 * x32, axis=-1, keepdims=True) + EPS)
    return (y * g.astype(jnp.float32)).astype(x.dtype)


def short_conv_mixer(xn, w_in, w_conv, w_out):
    L = xn.shape[1]
    b_gate, c_gate, h = jnp.split(xn @ w_in, 3, axis=-1)
    v = c_gate * h
    vp = jnp.pad(v, ((0, 0), (CONV_WIDTH - 1, 0), (0, 0)))
    y = w_conv[0] * vp[:, 0:L]
    for k in range(1, CONV_WIDTH):
        y = y + w_conv[k] * vp[:, k:k + L]
    return (b_gate * y) @ w_out


def stick_breaking_attention(xn, w_qkv, w_out):
    Bsz, L, _ = xn.shape
    pad = (-N_META) % Q_BLOCK
    qkv = (xn @ w_qkv).reshape(Bsz, L, 3, N_HEADS, HEAD_DIM)
    qkv = jnp.pad(qkv, ((0, 0), (pad, 0), (0, 0), (0, 0), (0, 0)))
    qkv = qkv.transpose(2, 0, 3, 1, 4)
    q, k, v = qkv[0], qkv[1], qkv[2]
    Lp = L + pad
    scale = HEAD_DIM ** -0.5
    outs = []
    for blk in range(Lp // Q_BLOCK):
        q0 = blk * Q_BLOCK
        kend = q0 + Q_BLOCK
        qb = q[:, :, q0:kend]
        kb = k[:, :, :kend]
        vb = v[:, :, :kend]
        z = jnp.einsum('bhqd,bhkd->bhqk', qb, kb).astype(jnp.float32) * scale
        t_pos = q0 + jnp.arange(Q_BLOCK)[:, None]
        s_pos = jnp.arange(kend)[None, :]
        mask = (s_pos < t_pos) & (s_pos >= pad)
        log_1m = jnp.where(mask, jax.nn.log_sigmoid(-z), 0.0)
        rest = lax.cumsum(log_1m, axis=3, reverse=True) - log_1m
        a = jnp.where(mask, jnp.exp(jax.nn.log_sigmoid(z) + rest), 0.0)
        outs.append(jnp.einsum('bhqk,bhkd->bhqd', a.astype(vb.dtype), vb))
    o = jnp.concatenate(outs, axis=2)[:, :, pad:]
    o = o.transpose(0, 2, 1, 3).reshape(Bsz, L, D_MODEL)
    return o @ w_out


def hierarchical_moe(xn, wg_r, bg_r, we_r, be_r, w_gate, w_up, w_down):
    Bsz, L, D = xn.shape
    xt = xn.reshape(-1, D)
    n_tok = xt.shape[0]
    group_logits = (xt @ wg_r).astype(jnp.float32) + bg_r.astype(jnp.float32)
    p_group = jax.nn.softmax(group_logits, axis=-1)
    p_top, g_sel = lax.top_k(p_group, 1)
    exp_logits = ((xt @ we_r).astype(jnp.float32) + be_r.astype(jnp.float32)).reshape(
        n_tok, N_GROUPS, EXPERTS_PER_GROUP)
    sel_logits = jnp.take_along_axis(exp_logits, g_sel[:, :, None], axis=1)[:, 0]
    top_val, top_idx = lax.top_k(sel_logits, TOP_K)
    gate = jax.nn.softmax(top_val, axis=-1) * p_top
    e_idx = g_sel * EXPERTS_PER_GROUP + top_idx
    combine = jnp.sum(jax.nn.one_hot(e_idx, N_EXPERTS, dtype=jnp.float32) * gate[..., None], axis=1)
    combine = combine.astype(xt.dtype)
    out = jnp.zeros_like(xt)
    for g in range(N_GROUPS):
        sl = slice(g * EXPERTS_PER_GROUP, (g + 1) * EXPERTS_PER_GROUP)
        hg = jax.nn.silu(jnp.einsum('nd,edf->nef', xt, w_gate[sl])) * jnp.einsum('nd,edf->nef', xt, w_up[sl])
        out = out + jnp.einsum('nef,efd->nd', hg * combine[:, sl, None], w_down[sl])
    return out.reshape(Bsz, L, D)


def setup_inputs(seed: int = 0) -> dict:
    key = jax.random.key(seed)
    ks = jax.random.split(key, 20)
    D, F, E, G = D_MODEL, D_EXPERT, N_EXPERTS, N_GROUPS
    nrm = jax.random.normal
    f32 = jnp.float32
    return {
        "x": nrm(ks[0], (BATCH, SEQ, D), f32),
        "meta_tokens": nrm(ks[1], (N_META, D), f32),
        "norm_mix_g": 1.0 + 0.02 * nrm(ks[2], (DEPTH, D), f32),
        "norm_ffn_g": 1.0 + 0.02 * nrm(ks[3], (DEPTH, D), f32),
        "conv_w_in": nrm(ks[4], (N_CONV_LAYERS, D, 3 * D), f32) * D ** -0.5,
        "conv_w": nrm(ks[5], (N_CONV_LAYERS, CONV_WIDTH, D), f32) * CONV_WIDTH ** -0.5,
        "conv_w_out": nrm(ks[6], (N_CONV_LAYERS, D, D), f32) * D ** -0.5,
        "attn_w_qkv": nrm(ks[7], (N_ATTN_LAYERS, D, 3 * D), f32) * D ** -0.5,
        "attn_w_out": nrm(ks[8], (N_ATTN_LAYERS, D, D), f32) * D ** -0.5,
        "router_group_w": nrm(ks[9], (DEPTH, D, G), f32) * D ** -0.5,
        "router_group_b": 0.01 * nrm(ks[10], (DEPTH, G), f32),
        "router_expert_w": nrm(ks[11], (DEPTH, D, E), f32) * D ** -0.5,
        "router_expert_b": 0.01 * nrm(ks[12], (DEPTH, E), f32),
        "moe_w_gate": nrm(ks[13], (DEPTH, E, D, F), f32) * D ** -0.5,
        "moe_w_up": nrm(ks[14], (DEPTH, E, D, F), f32) * D ** -0.5,
        "moe_w_down": nrm(ks[15], (DEPTH, E, F, D), f32) * F ** -0.5,
        "final_norm_g": 1.0 + 0.02 * nrm(ks[16], (D,), f32),
    }


def reference(x, meta_tokens, norm_mix_g, norm_ffn_g, conv_w_in, conv_w, conv_w_out,
              attn_w_qkv, attn_w_out, router_group_w, router_group_b, router_expert_w,
              router_expert_b, moe_w_gate, moe_w_up, moe_w_down, final_norm_g):
    Bsz = x.shape[0]
    meta = jnp.broadcast_to(meta_tokens[None].astype(x.dtype), (Bsz, N_META, D_MODEL))
    h = jnp.concatenate([meta, x], axis=1)
    for i in range(DEPTH):
        xn = rms_norm(h, norm_mix_g[i])
        j = i // N_MIXERS
        if i % N_MIXERS == 0:
            h = h + short_conv_mixer(xn, conv_w_in[j], conv_w[j], conv_w_out[j])
        else:
            h = h + stick_breaking_attention(xn, attn_w_qkv[j], attn_w_out[j])
        h = h + hierarchical_moe(rms_norm(h, norm_ffn_g[i]), router_group_w[i], router_group_b[i],
                                 router_expert_w[i], router_expert_b[i],
                                 moe_w_gate[i], moe_w_up[i], moe_w_down[i])
    return rms_norm(h, final_norm_g)[:, N_META:]
```

```python
import functools

import jax
import jax.numpy as jnp
from jax import lax
from jax.experimental import pallas as pl
from jax.experimental.pallas import tpu as pltpu

N_HEADS = 16
N_GROUPS = 4
TOP_K = 2
CONV_WIDTH = 3
EPS = 1e-6

META_BLOCK = 128
TAIL_ROWS = 512
MOE_TILE = 256
ROUTER_LANES = 128
SKIP_BELOW = -104.0
VMEM_LIMIT = 56 * 1024 * 1024

f32 = jnp.float32
bf16 = jnp.bfloat16


def _cparams(sem):
    return pltpu.CompilerParams(dimension_semantics=sem, vmem_limit_bytes=VMEM_LIMIT)


def _largest_divisor(n, candidates):
    return next(c for c in candidates if n % c == 0)


def _rms_scale(x):
    return x * lax.rsqrt(jnp.mean(x * x, axis=-1, keepdims=True) + EPS)


def _norm_matmul_kernel(h_ref, g_ref, w_ref, o_ref, xn_ref):
    @pl.when(pl.program_id(1) == 0)
    def _():
        xn_ref[...] = (_rms_scale(h_ref[...]) * g_ref[...]).astype(bf16)

    o_ref[...] = jnp.dot(xn_ref[...], w_ref[...], preferred_element_type=f32).astype(o_ref.dtype)


def _norm_matmul(h, g, w, *, tm, tn):
    rows, d = h.shape
    n = w.shape[1]
    return pl.pallas_call(
        _norm_matmul_kernel,
        grid=(rows // tm, n // tn),
        in_specs=[
            pl.BlockSpec((tm, d), lambda i, j: (i, 0)),
            pl.BlockSpec((1, d), lambda i, j: (0, 0)),
            pl.BlockSpec((d, tn), lambda i, j: (0, j)),
        ],
        out_specs=pl.BlockSpec((tm, tn), lambda i, j: (i, j)),
        out_shape=jax.ShapeDtypeStruct((rows, n), bf16),
        scratch_shapes=[pltpu.VMEM((tm, d), bf16)],
        compiler_params=_cparams(("arbitrary", "arbitrary")),
        name="norm_qkv",
    )(h, g, w)


def _conv_in_kernel(h_ref, g_ref, w_ref, cw_ref, o_ref, xn_ref, carry_ref, mcarry_ref, *,
                    tiles_per_seq):
    i = pl.program_id(0)
    j = pl.program_id(1)
    tm, tc = o_ref.shape

    @pl.when(j == 0)
    def _():
        xn_ref[...] = (_rms_scale(h_ref[...]) * g_ref[...]).astype(bf16)

    p = jnp.dot(xn_ref[...], w_ref[...], preferred_element_type=f32)
    b_gate = p[:, :tc]
    v = p[:, tc:2 * tc] * p[:, 2 * tc:]

    seq_start = (i >= 1) & (lax.rem(i - 1, tiles_per_seq) == 0)
    prev = jnp.where(i == 0, 0.0, jnp.where(seq_start, mcarry_ref[j], carry_ref[j]))
    p1 = prev[7:8, :]
    p2 = prev[6:7, :]
    row = lax.broadcasted_iota(jnp.int32, (tm, tc), 0)
    s1 = jnp.where(row == 0, p1, pltpu.roll(v, 1, 0))
    s2 = jnp.where(row == 0, p2, jnp.where(row == 1, p1, pltpu.roll(v, 2, 0)))
    y = cw_ref[0:1, :] * s2 + cw_ref[1:2, :] * s1 + cw_ref[2:3, :] * v
    o_ref[...] = (b_gate * y).astype(o_ref.dtype)

    carry_ref[j] = v[tm - 8:, :]

    @pl.when(i == 0)
    def _():
        mcarry_ref[j] = v[META_BLOCK - 8:META_BLOCK, :]


def _conv_in(h, g, w_perm, conv_w, *, seq_len, tm, tc):
    rows, d = h.shape
    n_i = rows // tm
    n_j = d // tc
    assert tm == TAIL_ROWS and seq_len % tm == 0

    def row_blk(i):
        return jnp.where(i == 0, n_i - 1, i - 1)

    kern = functools.partial(_conv_in_kernel, tiles_per_seq=seq_len // tm)
    return pl.pallas_call(
        kern,
        grid=(n_i, n_j),
        in_specs=[
            pl.BlockSpec((tm, d), lambda i, j: (row_blk(i), 0)),
            pl.BlockSpec((1, d), lambda i, j: (0, 0)),
            pl.BlockSpec((d, 3 * tc), lambda i, j: (0, j)),
            pl.BlockSpec((CONV_WIDTH, tc), lambda i, j: (0, j)),
        ],
        out_specs=pl.BlockSpec((tm, tc), lambda i, j: (row_blk(i), j)),
        out_shape=jax.ShapeDtypeStruct((rows, d), bf16),
        scratch_shapes=[
            pltpu.VMEM((tm, d), bf16),
            pltpu.VMEM((n_j, 8, tc), f32),
            pltpu.VMEM((n_j, 8, tc), f32),
        ],
        compiler_params=_cparams(("arbitrary", "arbitrary")),
        name="conv_in",
    )(h, g, w_perm, conv_w)


def _matmul_residual_kernel(a_ref, w_ref, r_ref, o_ref):
    o_ref[...] = r_ref[...] + jnp.dot(a_ref[...], w_ref[...], preferred_element_type=f32)


def _matmul_residual(a, w, r, *, tm):
    rows, k = a.shape
    n = w.shape[1]
    return pl.pallas_call(
        _matmul_residual_kernel,
        grid=(rows // tm,),
        in_specs=[
            pl.BlockSpec((tm, k), lambda i: (i, 0)),
            pl.BlockSpec((k, n), lambda i: (0, 0)),
            pl.BlockSpec((tm, n), lambda i: (i, 0)),
        ],
        out_specs=pl.BlockSpec((tm, n), lambda i: (i, 0)),
        out_shape=jax.ShapeDtypeStruct((rows, n), f32),
        compiler_params=_cparams(("arbitrary",)),
        name="out_proj_residual",
    )(a, w, r)


def _sb_block(q, kb, vb, u, carry, mask, scale):
    z = lax.dot_general(kb, q, (((1,), (1,)), ((), ())), preferred_element_type=f32) * scale
    softplus_neg = jnp.log1p(jnp.exp(-jnp.abs(z)))
    log_sig = jnp.minimum(z, 0.0) - softplus_neg
    log_1m = log_sig - z
    if mask is not None:
        log_1m = jnp.where(mask, log_1m, 0.0)
    hi = log_1m.astype(bf16)
    lo = (log_1m - hi.astype(f32)).astype(bf16)
    rest = (jnp.dot(u, hi, preferred_element_type=f32)
            + jnp.dot(u, lo, preferred_element_type=f32))
    a = jnp.exp(log_sig + rest + carry)
    if mask is not None:
        a = jnp.where(mask, a, 0.0)
    contrib = lax.dot_general(vb, a.astype(bf16), (((0,), (0,)), ((), ())),
                              preferred_element_type=f32)
    return contrib, carry + rest[0:1, :] + log_1m[0:1, :]


def _attn_kernel(q_ref, k_ref, v_ref, km_ref, vm_ref, u_ref, um_ref, o_ref, acc_ref, carry_ref, *,
                 scale, n_pad):
    i = pl.program_id(2)
    tq = q_ref.shape[0]
    tk = tq
    q = q_ref[...]

    s_loc = lax.broadcasted_iota(jnp.int32, (tk, tq), 0)
    t_loc = lax.broadcasted_iota(jnp.int32, (tk, tq), 1)
    start = pl.multiple_of(i * tk, tk)
    contrib, carry = _sb_block(q, k_ref[pl.ds(start, tk), :], v_ref[pl.ds(start, tk), :],
                               u_ref[...], jnp.zeros((1, tq), f32), s_loc < t_loc, scale)
    acc_ref[...] = contrib
    carry_ref[...] = carry

    def cond(state):
        j, top = state
        return (j >= 0) & (top > SKIP_BELOW)

    def body(state):
        j, _ = state
        st = pl.multiple_of(j * tk, tk)
        contrib, carry = _sb_block(q, k_ref[pl.ds(st, tk), :], v_ref[pl.ds(st, tk), :],
                                   u_ref[...], carry_ref[...], None, scale)
        acc_ref[...] += contrib
        carry_ref[...] = carry
        return j - 1, jnp.max(carry)

    _, top = lax.while_loop(cond, body, (i - 1, jnp.max(carry)))

    @pl.when(top > SKIP_BELOW)
    def _():
        mb = km_ref.shape[0]
        s_m = lax.broadcasted_iota(jnp.int32, (mb, tq), 0)
        contrib, _ = _sb_block(q, km_ref[...], vm_ref[...], um_ref[...], carry_ref[...],
                               s_m >= n_pad, scale)
        acc_ref[...] += contrib

    o_ref[...] = acc_ref[...].T.astype(o_ref.dtype)


def _attention(qkv, u, um, *, batch, seq_len, d_model, tq, n_pad):
    rows = qkv.shape[0]
    dh = d_model // N_HEADS
    nq = seq_len // tq
    meta_blk = (batch * seq_len) // META_BLOCK
    kern = functools.partial(_attn_kernel, scale=dh ** -0.5, n_pad=n_pad)
    return pl.pallas_call(
        kern,
        grid=(N_HEADS, batch, nq),
        in_specs=[
            pl.BlockSpec((tq, dh), lambda h, b, i: (b * nq + i, h)),
            pl.BlockSpec((seq_len, dh), lambda h, b, i: (b, N_HEADS + h)),
            pl.BlockSpec((seq_len, dh), lambda h, b, i: (b, 2 * N_HEADS + h)),
            pl.BlockSpec((META_BLOCK, dh), lambda h, b, i: (meta_blk, N_HEADS + h)),
            pl.BlockSpec((META_BLOCK, dh), lambda h, b, i: (meta_blk, 2 * N_HEADS + h)),
            pl.BlockSpec((tq, tq), lambda h, b, i: (0, 0)),
            pl.BlockSpec((META_BLOCK, META_BLOCK), lambda h, b, i: (0, 0)),
        ],
        out_specs=pl.BlockSpec((tq, dh), lambda h, b, i: (b * nq + i, h)),
        out_shape=jax.ShapeDtypeStruct((rows, d_model), bf16),
        scratch_shapes=[pltpu.VMEM((dh, tq), f32), pltpu.VMEM((1, tq), f32)],
        compiler_params=_cparams(("arbitrary", "arbitrary", "arbitrary")),
        name="sb_attention",
    )(qkv, qkv, qkv, qkv, qkv, u, um)


def _attn_tail_kernel(q_ref, k_ref, v_ref, um_ref, o_in_ref, o_ref, *, scale, n_pad):
    del o_in_ref
    mb = q_ref.shape[0]
    s_loc = lax.broadcasted_iota(jnp.int32, (mb, mb), 0)
    t_loc = lax.broadcasted_iota(jnp.int32, (mb, mb), 1)
    mask = (s_loc < t_loc) & (s_loc >= n_pad)
    contrib, _ = _sb_block(q_ref[...], k_ref[...], v_ref[...], um_ref[...],
                           jnp.zeros((1, mb), f32), mask, scale)
    o_ref[...] = jnp.zeros(o_ref.shape, o_ref.dtype)
    o_ref[0:mb, :] = contrib.T.astype(o_ref.dtype)


def _attention_tail(qkv, um, o, *, batch, seq_len, d_model, n_pad):
    dh = d_model // N_HEADS
    meta_blk = (batch * seq_len) // META_BLOCK
    tail_blk = (batch * seq_len) // TAIL_ROWS
    kern = functools.partial(_attn_tail_kernel, scale=dh ** -0.5, n_pad=n_pad)
    return pl.pallas_call(
        kern,
        grid=(N_HEADS,),
        in_specs=[
            pl.BlockSpec((META_BLOCK, dh), lambda h: (meta_blk, h)),
            pl.BlockSpec((META_BLOCK, dh), lambda h: (meta_blk, N_HEADS + h)),
            pl.BlockSpec((META_BLOCK, dh), lambda h: (meta_blk, 2 * N_HEADS + h)),
            pl.BlockSpec((META_BLOCK, META_BLOCK), lambda h: (0, 0)),
            pl.BlockSpec(memory_space=pl.ANY),
        ],
        out_specs=pl.BlockSpec((TAIL_ROWS, dh), lambda h: (tail_blk, h)),
        out_shape=jax.ShapeDtypeStruct(o.shape, o.dtype),
        input_output_aliases={4: 0},
        compiler_params=_cparams(("arbitrary",)),
        name="sb_attention_meta",
    )(qkv, qkv, qkv, um, o)


def _pack_bf16_pairs(x):
    half = x.shape[1] // 2
    lo = pltpu.bitcast(x[:, :half].astype(bf16).astype(f32), jnp.uint32)
    hi = pltpu.bitcast(x[:, half:].astype(bf16).astype(f32), jnp.uint32)
    return (hi & jnp.uint32(0xFFFF0000)) | (lo >> 16)


def _unpack_bf16_pairs(w):
    lo = pltpu.bitcast(w << 16, f32)
    hi = pltpu.bitcast(w & jnp.uint32(0xFFFF0000), f32)
    return jnp.concatenate([lo, hi], axis=1).astype(bf16)


def _router_kernel(h_ref, g_ref, w_ref, b_ref, xp_ref, info_ref, cnt_ref, base_ref, *, n_experts):
    i = pl.program_id(0)
    tm = h_ref.shape[0]
    epg = n_experts // N_GROUPS

    @pl.when(i == 0)
    def _():
        base_ref[...] = jnp.zeros(base_ref.shape, f32)

    xn = _rms_scale(h_ref[...]) * g_ref[...]
    xp_ref[...] = _pack_bf16_pairs(xn)

    logits = jnp.dot(xn, w_ref[...], preferred_element_type=f32,
                     precision=lax.Precision.HIGHEST) + b_ref[...]
    lane = lax.broadcasted_iota(jnp.int32, logits.shape, 1)
    neg = jnp.float32(-jnp.inf)
    big = jnp.int32(ROUTER_LANES)

    gl = jnp.where(lane < N_GROUPS, logits, neg)
    gmax = jnp.max(gl, axis=-1, keepdims=True)
    g_sel = jnp.min(jnp.where(gl == gmax, lane, big), axis=-1, keepdims=True)
    p_top = 1.0 / jnp.sum(jnp.exp(gl - gmax), axis=-1, keepdims=True)

    lo_lane = N_GROUPS + g_sel * epg
    el = jnp.where((lane >= lo_lane) & (lane < lo_lane + epg), logits, neg)
    v1 = jnp.max(el, axis=-1, keepdims=True)
    l1 = jnp.min(jnp.where(el == v1, lane, big), axis=-1, keepdims=True)
    el2 = jnp.where(lane == l1, neg, el)
    v2 = jnp.max(el2, axis=-1, keepdims=True)
    l2 = jnp.min(jnp.where(el2 == v2, lane, big), axis=-1, keepdims=True)
    r = jnp.exp(v2 - v1)
    gate1 = p_top * (1.0 / (1.0 + r))
    gate2 = p_top * (r / (1.0 + r))

    onehot = ((lane == l1) | (lane == l2)).astype(f32)
    r_i = lax.broadcasted_iota(jnp.int32, (tm, tm), 0)
    c_i = lax.broadcasted_iota(jnp.int32, (tm, tm), 1)
    earlier = jnp.where(c_i < r_i, 1.0, 0.0).astype(bf16)
    before = jnp.dot(earlier, onehot.astype(bf16), preferred_element_type=f32) + base_ref[...]
    rank1 = jnp.sum(jnp.where(lane == l1, before, 0.0), axis=-1, keepdims=True)
    rank2 = jnp.sum(jnp.where(lane == l2, before, 0.0), axis=-1, keepdims=True)
    base_ref[...] = base_ref[...] + jnp.sum(onehot, axis=0, keepdims=True)
    cnt_ref[...] = base_ref[...].astype(jnp.int32)

    e1 = jnp.clip(l1 - N_GROUPS, 0, n_experts - 1)
    e2 = jnp.clip(l2 - N_GROUPS, 0, n_experts - 1)
    info = jnp.where(lane == 0, e1, 0)
    info = jnp.where(lane == 1, e2, info)
    info = jnp.where(lane == 2, rank1.astype(jnp.int32), info)
    info = jnp.where(lane == 3, rank2.astype(jnp.int32), info)
    info = jnp.where(lane == 4, pltpu.bitcast(jnp.broadcast_to(gate1, logits.shape), jnp.int32), info)
    info = jnp.where(lane == 5, pltpu.bitcast(jnp.broadcast_to(gate2, logits.shape), jnp.int32), info)
    info_ref[...] = info


def _router(h, g, w_r, b_r, *, n_experts, tm):
    rows, d = h.shape
    kern = functools.partial(_router_kernel, n_experts=n_experts)
    return pl.pallas_call(
        kern,
        grid=(rows // tm,),
        in_specs=[
            pl.BlockSpec((tm, d), lambda i: (i, 0)),
            pl.BlockSpec((1, d), lambda i: (0, 0)),
            pl.BlockSpec((d, ROUTER_LANES), lambda i: (0, 0)),
            pl.BlockSpec((1, ROUTER_LANES), lambda i: (0, 0)),
        ],
        out_specs=[
            pl.BlockSpec((tm, d // 2), lambda i: (i, 0)),
            pl.BlockSpec((tm, ROUTER_LANES), lambda i: (i, 0)),
            pl.BlockSpec((1, ROUTER_LANES), lambda i: (0, 0)),
        ],
        out_shape=[
            jax.ShapeDtypeStruct((rows, d // 2), jnp.uint32),
            jax.ShapeDtypeStruct((rows, ROUTER_LANES), jnp.int32),
            jax.ShapeDtypeStruct((1, ROUTER_LANES), jnp.int32),
        ],
        scratch_shapes=[pltpu.VMEM((1, ROUTER_LANES), f32)],
        compiler_params=_cparams(("arbitrary",)),
        name="moe_router",
    )(h, g, w_r, b_r)


def _dispatch_kernel(p0_ref, p1_ref, x_ref, xs_ref, sem):
    i = pl.program_id(0)
    tm = x_ref.shape[0]
    base = i * tm

    def row_copy(r, dst_row):
        return pltpu.make_async_copy(x_ref.at[pl.ds(r, 1)], xs_ref.at[pl.ds(dst_row, 1)], sem)

    def issue(r, c):
        row_copy(r, p0_ref[base + r]).start()
        row_copy(r, p1_ref[base + r]).start()
        return c

    lax.fori_loop(0, tm, issue, 0)
    for _ in range(TOP_K):
        pltpu.make_async_copy(x_ref, xs_ref.at[pl.ds(0, tm)], sem).wait()


def _dispatch(pos0, pos1, xp, *, sorted_rows, tm):
    rows, dp = xp.shape
    return pl.pallas_call(
        _dispatch_kernel,
        grid_spec=pltpu.PrefetchScalarGridSpec(
            num_scalar_prefetch=2,
            grid=(rows // tm,),
            in_specs=[pl.BlockSpec((tm, dp), lambda i, p0, p1: (i, 0))],
            out_specs=pl.BlockSpec(memory_space=pl.ANY),
            scratch_shapes=[pltpu.SemaphoreType.DMA(())],
        ),
        out_shape=jax.ShapeDtypeStruct((sorted_rows, dp), jnp.uint32),
        compiler_params=pltpu.CompilerParams(dimension_semantics=("arbitrary",),
                                             vmem_limit_bytes=VMEM_LIMIT, has_side_effects=True),
        name="moe_dispatch",
    )(pos0, pos1, xp)


def _experts_kernel(te_ref, tv_ref, tf_ref, nu_ref, xs_ref, wg_ref, wu_ref, wd_ref, y_ref,
                    wg_b, wu_b, wd_b):
    t = pl.program_id(0)
    tm = xs_ref.shape[0]

    @pl.when(tf_ref[t] == 1)
    def _():
        wg_b[...] = wg_ref[0, 0].astype(bf16)
        wu_b[...] = wu_ref[0, 0].astype(bf16)
        wd_b[...] = wd_ref[0, 0].astype(bf16)

    @pl.when(t < nu_ref[0])
    def _():
        row = lax.broadcasted_iota(jnp.int32, xs_ref.shape, 0)
        xw = jnp.where(row < tv_ref[t], xs_ref[...], jnp.uint32(0))
        x = _unpack_bf16_pairs(xw)
        gate = jnp.dot(x, wg_b[...], preferred_element_type=f32)
        up = jnp.dot(x, wu_b[...], preferred_element_type=f32)
        hidden = (gate * (1.0 / (1.0 + jnp.exp(-gate))) * up).astype(bf16)
        y_ref[...] = jnp.dot(hidden, wd_b[...], preferred_element_type=f32)


def _experts(tile_expert, tile_valid, tile_first, n_used, xs, w_gate, w_up, w_down, *, layer,
             max_tiles):
    tm = MOE_TILE
    dp = xs.shape[1]
    _, _, d, f = w_gate.shape
    last_blk = max_tiles

    def x_map(t, te, tv, tf, nu):
        return (jnp.minimum(t, nu[0] - 1), 0)

    def y_map(t, te, tv, tf, nu):
        return (jnp.where(t < nu[0], t, last_blk), 0)

    return pl.pallas_call(
        _experts_kernel,
        grid_spec=pltpu.PrefetchScalarGridSpec(
            num_scalar_prefetch=4,
            grid=(max_tiles,),
            in_specs=[
                pl.BlockSpec((tm, dp), x_map),
                pl.BlockSpec((1, 1, d, f), lambda t, te, tv, tf, nu: (layer, te[t], 0, 0)),
                pl.BlockSpec((1, 1, d, f), lambda t, te, tv, tf, nu: (layer, te[t], 0, 0)),
                pl.BlockSpec((1, 1, f, d), lambda t, te, tv, tf, nu: (layer, te[t], 0, 0)),
            ],
            out_specs=pl.BlockSpec((tm, d), y_map),
            scratch_shapes=[pltpu.VMEM((d, f), bf16), pltpu.VMEM((d, f), bf16),
                            pltpu.VMEM((f, d), bf16)],
        ),
        out_shape=jax.ShapeDtypeStruct(((max_tiles + 1) * tm, d), f32),
        compiler_params=_cparams(("arbitrary",)),
        name="moe_experts",
    )(tile_expert, tile_valid, tile_first, n_used, xs, w_gate, w_up, w_down)


def _combine_kernel(p0_ref, p1_ref, h_ref, info_ref, gf_ref, ys_ref, o_ref, buf, sems, *,
                    n_steps, final_norm):
    i = pl.program_id(0)
    tm = h_ref.shape[0]

    def issue(step, slot):
        base = step * tm

        def body(r, c):
            pltpu.make_async_copy(ys_ref.at[pl.ds(p0_ref[base + r], 1)],
                                  buf.at[slot, 0, pl.ds(r, 1)], sems.at[slot]).start()
            pltpu.make_async_copy(ys_ref.at[pl.ds(p1_ref[base + r], 1)],
                                  buf.at[slot, 1, pl.ds(r, 1)], sems.at[slot]).start()
            return c

        lax.fori_loop(0, tm, body, 0)

    @pl.when(i == 0)
    def _():
        issue(0, 0)

    @pl.when(i + 1 < n_steps)
    def _():
        issue(i + 1, lax.rem(i + 1, 2))

    slot = lax.rem(i, 2)
    for k in range(TOP_K):
        pltpu.make_async_copy(ys_ref.at[pl.ds(0, tm)], buf.at[slot, k], sems.at[slot]).wait()

    info = info_ref[...]
    lane = lax.broadcasted_iota(jnp.int32, info.shape, 1)
    gates = pltpu.bitcast(info, f32)
    g1 = jnp.sum(jnp.where(lane == 4, gates, 0.0), axis=-1, keepdims=True)
    g2 = jnp.sum(jnp.where(lane == 5, gates, 0.0), axis=-1, keepdims=True)
    out = h_ref[...] + (g1 * buf[slot, 0] + g2 * buf[slot, 1])
    if final_norm:
        out = _rms_scale(out) * gf_ref[...]
    o_ref[...] = out


def _combine(pos0, pos1, h, info, gf, ys, *, out_rows, tm, final_norm):
    d = h.shape[1]
    n_steps = out_rows // tm
    kern = functools.partial(_combine_kernel, n_steps=n_steps, final_norm=final_norm)
    return pl.pallas_call(
        kern,
        grid_spec=pltpu.PrefetchScalarGridSpec(
            num_scalar_prefetch=2,
            grid=(n_steps,),
            in_specs=[
                pl.BlockSpec((tm, d), lambda i, p0, p1: (i, 0)),
                pl.BlockSpec((tm, ROUTER_LANES), lambda i, p0, p1: (i, 0)),
                pl.BlockSpec((1, d), lambda i, p0, p1: (0, 0)),
                pl.BlockSpec(memory_space=pl.ANY),
            ],
            out_specs=pl.BlockSpec((tm, d), lambda i, p0, p1: (i, 0)),
            scratch_shapes=[pltpu.VMEM((2, TOP_K, tm, d), f32), pltpu.SemaphoreType.DMA((2,))],
        ),
        out_shape=jax.ShapeDtypeStruct((out_rows, d), f32),
        compiler_params=_cparams(("arbitrary",)),
        name="moe_combine",
    )(pos0, pos1, h, info, gf, ys)


def _moe_layer(h, g_ffn, wg_r, bg_r, we_r, be_r, w_gate, w_up, w_down, gf, *, layer, out_rows,
               final_norm):
    rows, d = h.shape
    n_experts = we_r.shape[1]
    w_r = jnp.zeros((d, ROUTER_LANES), f32)
    w_r = w_r.at[:, :N_GROUPS].set(wg_r).at[:, N_GROUPS:N_GROUPS + n_experts].set(we_r)
    b_r = jnp.zeros((1, ROUTER_LANES), f32)
    b_r = b_r.at[0, :N_GROUPS].set(bg_r).at[0, N_GROUPS:N_GROUPS + n_experts].set(be_r)

    xp, info, cnt = _router(h, g_ffn.reshape(1, d), w_r, b_r, n_experts=n_experts, tm=512)

    tm = MOE_TILE
    max_tiles = (TOP_K * rows) // tm + n_experts
    cnt = cnt[0, N_GROUPS:N_GROUPS + n_experts]
    n_tiles = (cnt + tm - 1) // tm
    tile_end = jnp.cumsum(n_tiles)
    tile_start = tile_end - n_tiles
    n_used = tile_end[-1]
    t_ids = jnp.arange(max_tiles, dtype=jnp.int32)
    te = jnp.minimum(jnp.sum(tile_end[None, :] <= t_ids[:, None], axis=1), n_experts - 1)
    te = jnp.where(t_ids < n_used, te, te[jnp.maximum(n_used - 1, 0)]).astype(jnp.int32)
    used = t_ids < n_used
    tile_valid = jnp.where(used, jnp.clip(cnt[te] - (t_ids - tile_start[te]) * tm, 0, tm), 0)
    tile_first = (used & (t_ids == tile_start[te])).astype(jnp.int32)

    sorted_rows = (max_tiles + 1) * tm
    row_off = tile_start * tm
    pos0 = jnp.clip(row_off[info[:, 0]] + info[:, 2], 0, sorted_rows - 1).astype(jnp.int32)
    pos1 = jnp.clip(row_off[info[:, 1]] + info[:, 3], 0, sorted_rows - 1).astype(jnp.int32)

    xs = _dispatch(pos0, pos1, xp, sorted_rows=sorted_rows, tm=512)
    ys = _experts(te, tile_valid.astype(jnp.int32), tile_first,
                  n_used.reshape(1).astype(jnp.int32), xs, w_gate, w_up, w_down,
                  layer=layer, max_tiles=max_tiles)
    return _combine(pos0, pos1, h, info, gf.reshape(1, d), ys, out_rows=out_rows, tm=256,
                    final_norm=final_norm)


def _strict_upper_ones(n):
    r = lax.broadcasted_iota(jnp.int32, (n, n), 0)
    c = lax.broadcasted_iota(jnp.int32, (n, n), 1)
    return (c > r).astype(bf16)


def kernel(x, meta_tokens, norm_mix_g, norm_ffn_g, conv_w_in, conv_w, conv_w_out, attn_w_qkv, attn_w_out, router_group_w, router_group_b, router_expert_w, router_expert_b, moe_w_gate, moe_w_up, moe_w_down, final_norm_g):
    batch, seq_len, d = x.shape
    n_meta = meta_tokens.shape[0]
    n_pad = META_BLOCK - n_meta
    n_tok = batch * seq_len
    assert seq_len % TAIL_ROWS == 0 and d % (2 * N_HEADS) == 0

    h = jnp.concatenate([
        x.reshape(n_tok, d),
        jnp.zeros((n_pad, d), f32),
        meta_tokens.astype(f32),
        jnp.zeros((TAIL_ROWS - META_BLOCK, d), f32),
    ], axis=0)

    tc = _largest_divisor(d, (512, 256, 128))
    w_in = conv_w_in[0].astype(bf16)
    w_perm = jnp.concatenate(
        [w_in[:, k * d:(k + 1) * d].reshape(d, d // tc, 1, tc) for k in range(3)], axis=2
    ).reshape(d, 3 * d)
    gated = _conv_in(h, norm_mix_g[0].reshape(1, d), w_perm, conv_w[0], seq_len=seq_len,
                     tm=TAIL_ROWS, tc=tc)
    h = _matmul_residual(gated, conv_w_out[0].astype(bf16), h, tm=512)
    h = _moe_layer(h, norm_ffn_g[0], router_group_w[0], router_group_b[0], router_expert_w[0],
                   router_expert_b[0], moe_w_gate, moe_w_up, moe_w_down, final_norm_g,
                   layer=0, out_rows=h.shape[0], final_norm=False)

    qkv = _norm_matmul(h, norm_mix_g[1].reshape(1, d), attn_w_qkv[0].astype(bf16),
                       tm=_largest_divisor(h.shape[0], (768, 512)),
                       tn=_largest_divisor(3 * d, (1024, 768, 384)))
    tq = 256
    o = _attention(qkv, _strict_upper_ones(tq), _strict_upper_ones(META_BLOCK), batch=batch,
                   seq_len=seq_len, d_model=d, tq=tq, n_pad=n_pad)
    o = _attention_tail(qkv, _strict_upper_ones(META_BLOCK), o, batch=batch, seq_len=seq_len,
                        d_model=d, n_pad=n_pad)
    h = _matmul_residual(o, attn_w_out[0].astype(bf16), h, tm=512)
    out = _moe_layer(h, norm_ffn_g[1], router_group_w[1], router_group_b[1], router_expert_w[1],
                     router_expert_b[1], moe_w_gate, moe_w_up, moe_w_down, final_norm_g,
                     layer=1, out_rows=n_tok, final_norm=True)
    return out.reshape(batch, seq_len, d)
```

```python
import functools

import jax
import jax.numpy as jnp
from jax import lax
from jax.experimental import pallas as pl
from jax.experimental.pallas import tpu as pltpu

N_HEADS = 16
N_GROUPS = 4
TOP_K = 2
CONV_WIDTH = 3
EPS = 1e-6

LANES = 128
SUBLANES = 8
META_BLOCK = 128
TAIL_ROWS = 512
MOE_TILE = 256
ATT_BLOCK = 256
ATT_HEADS = 2
ATT_QSUB = 2
INFO_ROWS = 8
SKIP_BELOW = -104.0
LOG2E = 1.4426950408889634
VMEM_LIMIT = 56 * 1024 * 1024

f32 = jnp.float32
bf16 = jnp.bfloat16


def _cparams(sem):
    return pltpu.CompilerParams(dimension_semantics=sem, vmem_limit_bytes=VMEM_LIMIT)


def _largest_divisor(n, candidates):
    return next(c for c in candidates if n % c == 0)


def _rms_scale(x):
    return x * lax.rsqrt(jnp.mean(x * x, axis=-1, keepdims=True) + EPS)


def _norm_matmul_kernel(h_ref, g_ref, w_ref, o_ref, xn_ref):
    @pl.when(pl.program_id(1) == 0)
    def _():
        xn_ref[...] = (_rms_scale(h_ref[...]) * g_ref[...]).astype(bf16)

    o_ref[...] = jnp.dot(xn_ref[...], w_ref[...], preferred_element_type=f32).astype(o_ref.dtype)


def _norm_matmul(h, g, w, *, tm, tn):
    rows, d = h.shape
    n = w.shape[1]
    return pl.pallas_call(
        _norm_matmul_kernel,
        grid=(rows // tm, n // tn),
        in_specs=[
            pl.BlockSpec((tm, d), lambda i, j: (i, 0)),
            pl.BlockSpec((1, d), lambda i, j: (0, 0)),
            pl.BlockSpec((d, tn), lambda i, j: (0, j)),
        ],
        out_specs=pl.BlockSpec((tm, tn), lambda i, j: (i, j)),
        out_shape=jax.ShapeDtypeStruct((rows, n), bf16),
        scratch_shapes=[pltpu.VMEM((tm, d), bf16)],
        compiler_params=_cparams(("arbitrary", "arbitrary")),
        name="norm_qkv",
    )(h, g, w)


def _conv_in_kernel(x_ref, tail_ref, g_ref, wb_ref, wc_ref, wh_ref, cw_ref, o_ref, xn_ref,
                    carry_ref, mcarry_ref, *, tiles_per_seq):
    i = pl.program_id(0)
    j = pl.program_id(1)
    tm, tc = o_ref.shape

    @pl.when((j == 0) & (i == 0))
    def _():
        xn_ref[...] = (_rms_scale(tail_ref[...]) * g_ref[...]).astype(bf16)

    @pl.when((j == 0) & (i > 0))
    def _():
        xn_ref[...] = (_rms_scale(x_ref[...]) * g_ref[...]).astype(bf16)

    xn = xn_ref[...]
    b_gate = jnp.dot(xn, wb_ref[...], preferred_element_type=f32)
    v = (jnp.dot(xn, wc_ref[...], preferred_element_type=f32)
         * jnp.dot(xn, wh_ref[...], preferred_element_type=f32))

    seq_start = (i >= 1) & (lax.rem(i - 1, tiles_per_seq) == 0)
    prev = jnp.where(i == 0, 0.0, jnp.where(seq_start, mcarry_ref[j], carry_ref[j]))
    p1 = prev[7:8, :]
    p2 = prev[6:7, :]
    row = lax.broadcasted_iota(jnp.int32, (tm, tc), 0)
    s1 = jnp.where(row == 0, p1, pltpu.roll(v, 1, 0))
    s2 = jnp.where(row == 0, p2, jnp.where(row == 1, p1, pltpu.roll(v, 2, 0)))
    y = cw_ref[0:1, :] * s2 + cw_ref[1:2, :] * s1 + cw_ref[2:3, :] * v
    o_ref[...] = (b_gate * y).astype(o_ref.dtype)

    carry_ref[j] = v[tm - 8:, :]

    @pl.when(i == 0)
    def _():
        mcarry_ref[j] = v[META_BLOCK - 8:META_BLOCK, :]


def _conv_in(x2d, tail, g, w_in, conv_w, *, seq_len, tm, tc):
    n_tok, d = x2d.shape
    rows = n_tok + tail.shape[0]
    n_i = rows // tm
    n_j = d // tc
    assert tm == TAIL_ROWS == tail.shape[0] and seq_len % tm == 0

    def out_blk(i):
        return jnp.where(i == 0, n_i - 1, i - 1)

    kern = functools.partial(_conv_in_kernel, tiles_per_seq=seq_len // tm)
    return pl.pallas_call(
        kern,
        grid=(n_i, n_j),
        in_specs=[
            pl.BlockSpec((tm, d), lambda i, j: (jnp.maximum(i - 1, 0), 0)),
            pl.BlockSpec((tm, d), lambda i, j: (0, 0)),
            pl.BlockSpec((1, d), lambda i, j: (0, 0)),
            pl.BlockSpec((d, tc), lambda i, j: (0, j)),
            pl.BlockSpec((d, tc), lambda i, j: (0, n_j + j)),
            pl.BlockSpec((d, tc), lambda i, j: (0, 2 * n_j + j)),
            pl.BlockSpec((CONV_WIDTH, tc), lambda i, j: (0, j)),
        ],
        out_specs=pl.BlockSpec((tm, tc), lambda i, j: (out_blk(i), j)),
        out_shape=jax.ShapeDtypeStruct((rows, d), bf16),
        scratch_shapes=[
            pltpu.VMEM((tm, d), bf16),
            pltpu.VMEM((n_j, 8, tc), f32),
            pltpu.VMEM((n_j, 8, tc), f32),
        ],
        compiler_params=_cparams(("arbitrary", "arbitrary")),
        name="conv_in",
    )(x2d, tail, g, w_in, w_in, w_in, conv_w)


def _matmul_residual_kernel(a_ref, w_ref, r_ref, o_ref):
    o_ref[...] = r_ref[...] + jnp.dot(a_ref[...], w_ref[...], preferred_element_type=f32)


def _matmul_residual_split_kernel(a_ref, w_ref, r_ref, rt_ref, o_ref, *, n_main):
    mm = jnp.dot(a_ref[...], w_ref[...], preferred_element_type=f32)

    @pl.when(pl.program_id(0) < n_main)
    def _():
        o_ref[...] = r_ref[...] + mm

    @pl.when(pl.program_id(0) >= n_main)
    def _():
        o_ref[...] = rt_ref[...] + mm


def _matmul_residual(a, w, r, r_tail=None, *, tm):
    rows, k = a.shape
    n = w.shape[1]
    if r_tail is None:
        kern = _matmul_residual_kernel
        res_specs = [pl.BlockSpec((tm, n), lambda i: (i, 0))]
        res = (r,)
    else:
        n_main = r.shape[0] // tm
        assert r_tail.shape[0] == tm and r.shape[0] % tm == 0
        kern = functools.partial(_matmul_residual_split_kernel, n_main=n_main)
        res_specs = [pl.BlockSpec((tm, n), lambda i: (jnp.minimum(i, n_main - 1), 0)),
                     pl.BlockSpec((tm, n), lambda i: (0, 0))]
        res = (r, r_tail)
    return pl.pallas_call(
        kern,
        grid=(rows // tm,),
        in_specs=[pl.BlockSpec((tm, k), lambda i: (i, 0)),
                  pl.BlockSpec((k, n), lambda i: (0, 0))] + res_specs,
        out_specs=pl.BlockSpec((tm, n), lambda i: (i, 0)),
        out_shape=jax.ShapeDtypeStruct((rows, n), f32),
        compiler_params=_cparams(("arbitrary",)),
        name="out_proj_residual",
    )(a, w, *res)


def _sb_block(q, kb, vb, u, carry, mask, scale):
    z = lax.dot_general(kb, q, (((1,), (1,)), ((), ())), preferred_element_type=f32) * scale
    softplus_neg = jnp.log(1.0 + jnp.exp2(jnp.abs(z) * (-LOG2E)))
    log_sig = jnp.minimum(z, 0.0) - softplus_neg
    log_1m = log_sig - z
    if mask is not None:
        log_1m = jnp.where(mask, log_1m, 0.0)
    hi = log_1m.astype(bf16)
    lo = (log_1m - hi.astype(f32)).astype(bf16)
    tq = q.shape[0]
    both = jnp.dot(u, jnp.concatenate([hi, lo], axis=1), preferred_element_type=f32)
    rest = both[:, :tq] + both[:, tq:]
    a = jnp.exp2((log_sig + rest + carry) * LOG2E)
    if mask is not None:
        a = jnp.where(mask, a, 0.0)
    contrib = lax.dot_general(vb, a.astype(bf16), (((0,), (0,)), ((), ())),
                              preferred_element_type=f32)
    return contrib, carry + rest[0:1, :] + log_1m[0:1, :]


def _attn_kernel(q_ref, k_ref, v_ref, km_ref, vm_ref, u_ref, o_ref, acc_ref, carry_ref, *,
                 scale, n_pad, n_main_steps):
    step = pl.program_id(1)
    steps_per_seq = k_ref.shape[0] // (ATT_QSUB * ATT_BLOCK)
    qstep = lax.rem(step, steps_per_seq)
    tb = ATT_BLOCK
    dh = q_ref.shape[1] // ATT_HEADS
    u = u_ref[...]
    s_loc = lax.broadcasted_iota(jnp.int32, (tb, tb), 0)
    t_loc = lax.broadcasted_iota(jnp.int32, (tb, tb), 1)
    causal = s_loc < t_loc
    meta_keys = (s_loc >= n_pad) & (s_loc < META_BLOCK)

    @pl.when(step < n_main_steps)
    def _():
        chains = [(hh, qs) for hh in range(ATT_HEADS) for qs in range(ATT_QSUB)]
        for hh, qs in chains:
            cols = slice(hh * dh, (hh + 1) * dh)
            i = qstep * ATT_QSUB + qs
            q = q_ref[qs * tb:(qs + 1) * tb, cols]
            start = pl.multiple_of(i * tb, tb)
            contrib, carry = _sb_block(q, k_ref[pl.ds(start, tb), cols], v_ref[pl.ds(start, tb), cols],
                                       u, jnp.zeros((1, tb), f32), causal, scale)
            pstart = pl.multiple_of(jnp.maximum(i - 1, 0) * tb, tb)
            kp = k_ref[pl.ds(pstart, tb), cols]
            vp = v_ref[pl.ds(pstart, tb), cols]
            if qs == 0:
                first = qstep == 0
                kp = jnp.where(first, km_ref[:, cols], kp)
                vp = jnp.where(first, vm_ref[:, cols], vp)
                pmask = meta_keys | jnp.logical_not(first)
            else:
                pmask = None
            contrib2, carry = _sb_block(q, kp, vp, u, carry, pmask, scale)
            acc_ref[hh, qs] = contrib + contrib2
            carry_ref[hh, qs] = carry

        for hh, qs in chains:
            cols = slice(hh * dh, (hh + 1) * dh)
            i = qstep * ATT_QSUB + qs

            def cond(state):
                j, top = state
                return (j >= 0) & (top > SKIP_BELOW)

            def body(state, hh=hh, qs=qs, cols=cols):
                j, _ = state
                st = pl.multiple_of(j * tb, tb)
                q = q_ref[qs * tb:(qs + 1) * tb, cols]
                contrib, carry = _sb_block(q, k_ref[pl.ds(st, tb), cols], v_ref[pl.ds(st, tb), cols],
                                           u, carry_ref[hh, qs], None, scale)
                acc_ref[hh, qs] += contrib
                carry_ref[hh, qs] = carry
                return j - 1, jnp.max(carry)

            _, top = lax.while_loop(cond, body, (i - 2, jnp.max(carry_ref[hh, qs])))

            @pl.when((top > SKIP_BELOW) & (i >= 1))
            def _(hh=hh, qs=qs, cols=cols):
                q = q_ref[qs * tb:(qs + 1) * tb, cols]
                contrib, _ = _sb_block(q, km_ref[:, cols], vm_ref[:, cols], u, carry_ref[hh, qs],
                                       meta_keys, scale)
                acc_ref[hh, qs] += contrib

        for hh, qs in chains:
            o_ref[qs * tb:(qs + 1) * tb, hh * dh:(hh + 1) * dh] = acc_ref[hh, qs].T.astype(o_ref.dtype)

    @pl.when(step >= n_main_steps)
    def _():
        o_ref[...] = jnp.zeros(o_ref.shape, o_ref.dtype)
        for hh in range(ATT_HEADS):
            cols = slice(hh * dh, (hh + 1) * dh)
            contrib, _ = _sb_block(q_ref[0:tb, cols], km_ref[:, cols], vm_ref[:, cols], u,
                                   jnp.zeros((1, tb), f32), causal & meta_keys, scale)
            o_ref[0:tb, cols] = contrib.T.astype(o_ref.dtype)


def _attention(qkv, u, *, batch, seq_len, d_model, n_pad):
    rows = qkv.shape[0]
    dh = d_model // N_HEADS
    tqs = ATT_QSUB * ATT_BLOCK
    assert tqs == TAIL_ROWS and seq_len % tqs == 0 and N_HEADS % ATT_HEADS == 0
    steps_per_seq = seq_len // tqs
    n_main = batch * steps_per_seq
    n_hp = N_HEADS // ATT_HEADS
    wcols = ATT_HEADS * dh
    meta_blk = (batch * seq_len) // ATT_BLOCK
    kern = functools.partial(_attn_kernel, scale=dh ** -0.5, n_pad=n_pad, n_main_steps=n_main)

    def seq_of(s):
        return jnp.minimum(s // steps_per_seq, batch - 1)

    return pl.pallas_call(
        kern,
        grid=(n_hp, n_main + 1),
        in_specs=[
            pl.BlockSpec((tqs, wcols), lambda h, s: (s, h)),
            pl.BlockSpec((seq_len, wcols), lambda h, s: (seq_of(s), n_hp + h)),
            pl.BlockSpec((seq_len, wcols), lambda h, s: (seq_of(s), 2 * n_hp + h)),
            pl.BlockSpec((ATT_BLOCK, wcols), lambda h, s: (meta_blk, n_hp + h)),
            pl.BlockSpec((ATT_BLOCK, wcols), lambda h, s: (meta_blk, 2 * n_hp + h)),
            pl.BlockSpec((ATT_BLOCK, ATT_BLOCK), lambda h, s: (0, 0)),
        ],
        out_specs=pl.BlockSpec((tqs, wcols), lambda h, s: (s, h)),
        out_shape=jax.ShapeDtypeStruct((rows, d_model), bf16),
        scratch_shapes=[pltpu.VMEM((ATT_HEADS, ATT_QSUB, dh, ATT_BLOCK), f32),
                        pltpu.VMEM((ATT_HEADS, ATT_QSUB, 1, ATT_BLOCK), f32)],
        compiler_params=_cparams(("arbitrary", "arbitrary")),
        name="sb_attention",
    )(qkv, qkv, qkv, qkv, qkv, u)


def _pack_rows(x, w_ref, z_ref):
    m, d = x.shape
    half = d // 2
    for c in range(half // LANES):
        z_ref[pl.ds(0, m, stride=2), :] = x[:, c * LANES:(c + 1) * LANES]
        z_ref[pl.ds(1, m, stride=2), :] = x[:, half + c * LANES:half + (c + 1) * LANES]
        words = pltpu.bitcast(z_ref[...].astype(bf16), jnp.uint32)
        w_ref[:, c, :, :] = words.reshape(m // SUBLANES, SUBLANES, LANES)


def _unpack_rows(w_ref, z_ref):
    m = w_ref.shape[0] * SUBLANES
    lo, hi = [], []
    for c in range(w_ref.shape[1]):
        words = w_ref[:, c, :, :].reshape(m, LANES)
        z_ref[...] = pltpu.bitcast(words, bf16).astype(f32)
        lo.append(z_ref[pl.ds(0, m, stride=2), :].astype(bf16))
        hi.append(z_ref[pl.ds(1, m, stride=2), :].astype(bf16))
    return jnp.concatenate(lo + hi, axis=1)


def _row_tiled(rows, width):
    return (rows // SUBLANES, width // LANES, SUBLANES, LANES)


def _row_of(ref, r):
    return ref.at[lax.shift_right_logical(r, 3), :, pl.ds(r & (SUBLANES - 1), 1), :]


def _router_kernel(h_ref, g_ref, w_ref, b_ref, xp_ref, info_ref, infot_ref, cnt_ref, base_ref,
                   z_ref, *, n_experts):
    i = pl.program_id(0)
    tm = h_ref.shape[0]
    epg = n_experts // N_GROUPS

    @pl.when(i == 0)
    def _():
        base_ref[...] = jnp.zeros(base_ref.shape, f32)

    xn = _rms_scale(h_ref[...]) * g_ref[...]
    _pack_rows(xn, xp_ref, z_ref)

    logits = jnp.dot(xn, w_ref[...], preferred_element_type=f32,
                     precision=lax.Precision.HIGHEST) + b_ref[...]
    lane = lax.broadcasted_iota(jnp.int32, logits.shape, 1)
    neg = jnp.float32(-jnp.inf)
    big = jnp.int32(LANES)

    gl = jnp.where(lane < N_GROUPS, logits, neg)
    gmax = jnp.max(gl, axis=-1, keepdims=True)
    g_sel = jnp.min(jnp.where(gl == gmax, lane, big), axis=-1, keepdims=True)
    p_top = 1.0 / jnp.sum(jnp.exp(gl - gmax), axis=-1, keepdims=True)

    lo_lane = N_GROUPS + g_sel * epg
    el = jnp.where((lane >= lo_lane) & (lane < lo_lane + epg), logits, neg)
    v1 = jnp.max(el, axis=-1, keepdims=True)
    l1 = jnp.min(jnp.where(el == v1, lane, big), axis=-1, keepdims=True)
    el2 = jnp.where(lane == l1, neg, el)
    v2 = jnp.max(el2, axis=-1, keepdims=True)
    l2 = jnp.min(jnp.where(el2 == v2, lane, big), axis=-1, keepdims=True)
    r = jnp.exp(v2 - v1)
    gate1 = p_top * (1.0 / (1.0 + r))
    gate2 = p_top * (r / (1.0 + r))

    onehot = ((lane == l1) | (lane == l2)).astype(f32)
    r_i = lax.broadcasted_iota(jnp.int32, (tm, tm), 0)
    c_i = lax.broadcasted_iota(jnp.int32, (tm, tm), 1)
    earlier = jnp.where(c_i < r_i, 1.0, 0.0).astype(bf16)
    before = jnp.dot(earlier, onehot.astype(bf16), preferred_element_type=f32) + base_ref[...]
    rank1 = jnp.sum(jnp.where(lane == l1, before, 0.0), axis=-1, keepdims=True)
    rank2 = jnp.sum(jnp.where(lane == l2, before, 0.0), axis=-1, keepdims=True)
    base_ref[...] = base_ref[...] + jnp.sum(onehot, axis=0, keepdims=True)
    cnt_ref[...] = base_ref[...].astype(jnp.int32)

    e1 = jnp.clip(l1 - N_GROUPS, 0, n_experts - 1)
    e2 = jnp.clip(l2 - N_GROUPS, 0, n_experts - 1)
    info = jnp.where(lane == 0, e1, 0)
    info = jnp.where(lane == 1, e2, info)
    info = jnp.where(lane == 2, rank1.astype(jnp.int32), info)
    info = jnp.where(lane == 3, rank2.astype(jnp.int32), info)
    info = jnp.where(lane == 4, pltpu.bitcast(jnp.broadcast_to(gate1, logits.shape), jnp.int32), info)
    info = jnp.where(lane == 5, pltpu.bitcast(jnp.broadcast_to(gate2, logits.shape), jnp.int32), info)
    info_ref[...] = info
    infot_ref[...] = info.T[0:INFO_ROWS, :]


def _router(h, g, w_r, b_r, *, n_experts, tm):
    rows, d = h.shape
    kern = functools.partial(_router_kernel, n_experts=n_experts)
    return pl.pallas_call(
        kern,
        grid=(rows // tm,),
        in_specs=[
            pl.BlockSpec((tm, d), lambda i: (i, 0)),
            pl.BlockSpec((1, d), lambda i: (0, 0)),
            pl.BlockSpec((d, LANES), lambda i: (0, 0)),
            pl.BlockSpec((1, LANES), lambda i: (0, 0)),
        ],
        out_specs=[
            pl.BlockSpec(_row_tiled(tm, d // 2), lambda i: (i, 0, 0, 0)),
            pl.BlockSpec((tm, LANES), lambda i: (i, 0)),
            pl.BlockSpec((INFO_ROWS, tm), lambda i: (0, i)),
            pl.BlockSpec((1, LANES), lambda i: (0, 0)),
        ],
        out_shape=[
            jax.ShapeDtypeStruct(_row_tiled(rows, d // 2), jnp.uint32),
            jax.ShapeDtypeStruct((rows, LANES), jnp.int32),
            jax.ShapeDtypeStruct((INFO_ROWS, rows), jnp.int32),
            jax.ShapeDtypeStruct((1, LANES), jnp.int32),
        ],
        scratch_shapes=[pltpu.VMEM((1, LANES), f32), pltpu.VMEM((2 * tm, LANES), f32)],
        compiler_params=_cparams(("arbitrary",)),
        name="moe_router",
    )(h, g, w_r, b_r)


def _pad_fill_copies(fs_ref, pad_ref, zbuf, xs_ref, sem, e):
    fs = fs_ref[e]
    pad = pad_ref[e]
    head = pad & (SUBLANES - 1)
    out = []
    for k in range(SUBLANES - 1):
        out.append((k < head,
                    pltpu.make_async_copy(zbuf.at[0, :, pl.ds(0, 1), :], _row_of(xs_ref, fs + k), sem)))
    cur = lax.shift_right_logical(fs + head, 3)
    for b in reversed(range(SUBLANES.bit_length() - 1, MOE_TILE.bit_length() - 1)):
        size = 1 << b
        groups = size // SUBLANES
        out.append(((pad & size) != 0,
                    pltpu.make_async_copy(zbuf.at[pl.ds(0, groups)], xs_ref.at[pl.ds(cur, groups)], sem)))
        cur = cur + lax.shift_right_logical(pad & size, 3)
    return out


def _unused_tile_copies(zbuf, xs_ref, sem, t):
    groups = zbuf.shape[0]
    per_tile = MOE_TILE // SUBLANES
    return [pltpu.make_async_copy(zbuf, xs_ref.at[pl.ds(t * per_tile + k * groups, groups)], sem)
            for k in range(per_tile // groups)]


def _dispatch_kernel(p0_ref, p1_ref, fs_ref, pad_ref, nu_ref, x_ref, xs_ref, zbuf, sem, zsem, *,
                     n_steps, n_experts, max_tiles):
    i = pl.program_id(0)
    tm = x_ref.shape[0] * SUBLANES
    base = i * tm

    @pl.when(i == 0)
    def _():
        zbuf[...] = jnp.zeros(zbuf.shape, zbuf.dtype)

        def fill(e, c):
            for needed, copy in _pad_fill_copies(fs_ref, pad_ref, zbuf, xs_ref, zsem, e):
                @pl.when(needed)
                def _():
                    copy.start()
            return c

        lax.fori_loop(0, n_experts, fill, 0)

        def fill_unused(t, c):
            for copy in _unused_tile_copies(zbuf, xs_ref, zsem, t):
                copy.start()
            return c

        lax.fori_loop(nu_ref[0], max_tiles, fill_unused, 0)

    def issue(g, c):
        for k in range(SUBLANES):
            src = x_ref.at[g, :, pl.ds(k, 1), :]
            r = base + g * SUBLANES + k
            pltpu.make_async_copy(src, _row_of(xs_ref, p0_ref[r]), sem).start(priority=0)
            pltpu.make_async_copy(src, _row_of(xs_ref, p1_ref[r]), sem).start(priority=1)
        return c

    lax.fori_loop(0, tm // SUBLANES, issue, 0)
    for _ in range(TOP_K):
        pltpu.make_async_copy(x_ref, xs_ref.at[pl.ds(0, tm // SUBLANES)], sem).wait()

    @pl.when(i == n_steps - 1)
    def _():
        def drain(e, c):
            for needed, copy in _pad_fill_copies(fs_ref, pad_ref, zbuf, xs_ref, zsem, e):
                @pl.when(needed)
                def _():
                    copy.wait()
            return c

        lax.fori_loop(0, n_experts, drain, 0)

        def drain_unused(t, c):
            for copy in _unused_tile_copies(zbuf, xs_ref, zsem, t):
                copy.wait()
            return c

        lax.fori_loop(nu_ref[0], max_tiles, drain_unused, 0)


def _dispatch(pos0, pos1, fill_start, pad, n_used, xp, *, sorted_rows, tm):
    width = xp.shape[1] * LANES
    n_steps = (xp.shape[0] * SUBLANES) // tm
    kern = functools.partial(_dispatch_kernel, n_steps=n_steps, n_experts=pad.shape[0],
                             max_tiles=sorted_rows // MOE_TILE)
    return pl.pallas_call(
        kern,
        grid_spec=pltpu.PrefetchScalarGridSpec(
            num_scalar_prefetch=5,
            grid=(n_steps,),
            in_specs=[pl.BlockSpec(_row_tiled(tm, width), lambda i, *_: (i, 0, 0, 0))],
            out_specs=pl.BlockSpec(memory_space=pl.ANY),
            scratch_shapes=[pltpu.VMEM(_row_tiled(MOE_TILE // 2, width), jnp.uint32),
                            pltpu.SemaphoreType.DMA(()), pltpu.SemaphoreType.DMA(())],
        ),
        out_shape=jax.ShapeDtypeStruct(_row_tiled(sorted_rows, width), jnp.uint32),
        compiler_params=pltpu.CompilerParams(dimension_semantics=("arbitrary",),
                                             vmem_limit_bytes=VMEM_LIMIT, has_side_effects=True),
        name="moe_dispatch",
    )(pos0, pos1, fill_start, pad, n_used, xp)


def _experts_kernel(te_ref, tf_ref, nx_ref, nu_ref, xs_ref, wg_hbm, wu_hbm, wd_hbm, y_ref,
                    wg_f, wu_f, wd_f, wg_b, wu_b, wd_b, z_ref, slot_ref, sems, *, layer):
    t = pl.program_id(0)

    def weight_copies(e, slot):
        return [pltpu.make_async_copy(hbm.at[layer, e], buf.at[slot], sems.at[slot, k])
                for k, (hbm, buf) in enumerate(((wg_hbm, wg_f), (wu_hbm, wu_f), (wd_hbm, wd_f)))]

    @pl.when(t == 0)
    def _():
        slot_ref[0] = 0
        for c in weight_copies(te_ref[0], 0):
            c.start()

    @pl.when(tf_ref[t] == 1)
    def _():
        slot = slot_ref[0]
        for c in weight_copies(te_ref[t], slot):
            c.wait()

        @pl.when(nx_ref[t] >= 0)
        def _():
            for c in weight_copies(nx_ref[t], 1 - slot):
                c.start()

        wg_b[...] = wg_f[slot].astype(bf16)
        wu_b[...] = wu_f[slot].astype(bf16)
        wd_b[...] = wd_f[slot].astype(bf16)
        slot_ref[0] = 1 - slot

    @pl.when(t < nu_ref[0])
    def _():
        x = _unpack_rows(xs_ref, z_ref)
        gate = jnp.dot(x, wg_b[...], preferred_element_type=f32)
        up = jnp.dot(x, wu_b[...], preferred_element_type=f32)
        hidden = (gate * (1.0 / (1.0 + jnp.exp(-gate))) * up).astype(bf16)
        y = jnp.dot(hidden, wd_b[...], preferred_element_type=f32)
        for c in range(y_ref.shape[1]):
            y_ref[:, c, :, :] = y[:, c * LANES:(c + 1) * LANES].reshape(y_ref.shape[0], SUBLANES, LANES)

    @pl.when(t >= nu_ref[0])
    def _():
        y_ref[...] = jnp.zeros(y_ref.shape, y_ref.dtype)


def _experts(tile_expert, tile_first, next_expert, n_used, xs, w_gate, w_up, w_down, *, layer,
             max_tiles):
    tm = MOE_TILE
    _, _, d, f = w_gate.shape
    kern = functools.partial(_experts_kernel, layer=layer)
    hbm = pl.BlockSpec(memory_space=pl.ANY)
    return pl.pallas_call(
        kern,
        grid_spec=pltpu.PrefetchScalarGridSpec(
            num_scalar_prefetch=4,
            grid=(max_tiles,),
            in_specs=[pl.BlockSpec(_row_tiled(tm, d // 2),
                                   lambda t, te, tf, nx, nu: (jnp.minimum(t, nu[0] - 1), 0, 0, 0)),
                      hbm, hbm, hbm],
            out_specs=pl.BlockSpec(_row_tiled(tm, d), lambda t, *_: (t, 0, 0, 0)),
            scratch_shapes=[pltpu.VMEM((2, d, f), f32), pltpu.VMEM((2, d, f), f32),
                            pltpu.VMEM((2, f, d), f32),
                            pltpu.VMEM((d, f), bf16), pltpu.VMEM((d, f), bf16),
                            pltpu.VMEM((f, d), bf16), pltpu.VMEM((2 * tm, LANES), f32),
                            pltpu.SMEM((1,), jnp.int32), pltpu.SemaphoreType.DMA((2, 3))],
        ),
        out_shape=jax.ShapeDtypeStruct(_row_tiled(max_tiles * tm, d), f32),
        compiler_params=_cparams(("arbitrary",)),
        name="moe_experts",
    )(tile_expert, tile_first, next_expert, n_used, xs, w_gate, w_up, w_down)


def _combine_kernel(p0_ref, p1_ref, h_ref, info_ref, gf_ref, ys_ref, o_ref, buf, sems, *,
                    n_steps, final_norm):
    i = pl.program_id(0)
    tm = h_ref.shape[0]

    def issue(step, slot):
        base = step * tm

        def body(g, c):
            for k in range(SUBLANES):
                r = base + g * SUBLANES + k
                pltpu.make_async_copy(_row_of(ys_ref, p0_ref[r]),
                                      buf.at[slot, 0, g, :, pl.ds(k, 1), :],
                                      sems.at[slot]).start(priority=0)
                pltpu.make_async_copy(_row_of(ys_ref, p1_ref[r]),
                                      buf.at[slot, 1, g, :, pl.ds(k, 1), :],
                                      sems.at[slot]).start(priority=1)
            return c

        lax.fori_loop(0, tm // SUBLANES, body, 0)

    @pl.when(i == 0)
    def _():
        issue(0, 0)

    @pl.when(i + 1 < n_steps)
    def _():
        issue(i + 1, lax.rem(i + 1, 2))

    slot = lax.rem(i, 2)
    for k in range(TOP_K):
        pltpu.make_async_copy(ys_ref.at[pl.ds(0, tm // SUBLANES)], buf.at[slot, k],
                              sems.at[slot]).wait()

    info = info_ref[...]
    lane = lax.broadcasted_iota(jnp.int32, info.shape, 1)
    gates = pltpu.bitcast(info, f32)
    g1 = jnp.sum(jnp.where(lane == 4, gates, 0.0), axis=-1, keepdims=True)
    g2 = jnp.sum(jnp.where(lane == 5, gates, 0.0), axis=-1, keepdims=True)
    cols = [g1 * buf[slot, 0, :, c, :, :].reshape(tm, LANES)
            + g2 * buf[slot, 1, :, c, :, :].reshape(tm, LANES) for c in range(buf.shape[3])]
    out = h_ref[...] + jnp.concatenate(cols, axis=1)
    if final_norm:
        out = _rms_scale(out) * gf_ref[...]
    o_ref[...] = out


def _combine(pos0, pos1, h, info, gf, ys, *, out_rows, tm, final_norm):
    d = h.shape[1]
    n_steps = out_rows // tm
    kern = functools.partial(_combine_kernel, n_steps=n_steps, final_norm=final_norm)
    return pl.pallas_call(
        kern,
        grid_spec=pltpu.PrefetchScalarGridSpec(
            num_scalar_prefetch=2,
            grid=(n_steps,),
            in_specs=[
                pl.BlockSpec((tm, d), lambda i, p0, p1: (i, 0)),
                pl.BlockSpec((tm, LANES), lambda i, p0, p1: (i, 0)),
                pl.BlockSpec((1, d), lambda i, p0, p1: (0, 0)),
                pl.BlockSpec(memory_space=pl.ANY),
            ],
            out_specs=pl.BlockSpec((tm, d), lambda i, p0, p1: (i, 0)),
            scratch_shapes=[pltpu.VMEM((2, TOP_K) + _row_tiled(tm, d), f32),
                            pltpu.SemaphoreType.DMA((2,))],
        ),
        out_shape=jax.ShapeDtypeStruct((out_rows, d), f32),
        compiler_params=_cparams(("arbitrary",)),
        name="moe_combine",
    )(pos0, pos1, h, info, gf, ys)


def _table_lookup(table, idx):
    k = lax.broadcasted_iota(jnp.int32, (table.shape[0], idx.shape[0]), 0)
    return jnp.sum(jnp.where(k == idx[None, :], table[:, None], 0), axis=0)


def _moe_layer(h, g_ffn, wg_r, bg_r, we_r, be_r, w_gate, w_up, w_down, gf, *, layer, out_rows,
               final_norm):
    rows, d = h.shape
    n_experts = we_r.shape[1]
    n_fill = LANES - N_GROUPS - n_experts
    w_r = jnp.concatenate([wg_r, we_r, jnp.zeros((d, n_fill), f32)], axis=1)
    b_r = jnp.concatenate([bg_r, be_r, jnp.zeros((n_fill,), f32)]).reshape(1, LANES)

    xp, info, info_t, cnt = _router(h, g_ffn.reshape(1, d), w_r, b_r, n_experts=n_experts, tm=512)

    tm = MOE_TILE
    max_tiles = (TOP_K * rows) // tm + n_experts
    i32 = jnp.int32
    cnt = cnt[0, N_GROUPS:N_GROUPS + n_experts]
    n_tiles = (cnt + tm - 1) // tm
    tile_end = jnp.cumsum(n_tiles)
    tile_start = tile_end - n_tiles
    n_used = tile_end[-1]
    t_ids = jnp.arange(max_tiles, dtype=i32)
    used = t_ids < n_used
    te = jnp.minimum(jnp.sum(tile_end[None, :] <= t_ids[:, None], axis=1), n_experts - 1)
    tile_first = (used & (t_ids == tile_start[te])).astype(i32)
    nxt_tile = tile_end[te]
    next_expert = jnp.where(nxt_tile < n_used, te[jnp.minimum(nxt_tile, max_tiles - 1)], -1)

    sorted_rows = max_tiles * tm
    row_off = (tile_start * tm).astype(i32)
    pos0 = jnp.clip(_table_lookup(row_off, info_t[0]) + info_t[2], 0, sorted_rows - 1)
    pos1 = jnp.clip(_table_lookup(row_off, info_t[1]) + info_t[3], 0, sorted_rows - 1)

    n_used = n_used.reshape(1).astype(i32)
    xs = _dispatch(pos0, pos1, (row_off + cnt).astype(i32), (n_tiles * tm - cnt).astype(i32), n_used,
                   xp, sorted_rows=sorted_rows, tm=512)
    ys = _experts(te.astype(i32), tile_first, next_expert.astype(i32), n_used,
                  xs, w_gate, w_up, w_down, layer=layer, max_tiles=max_tiles)
    return _combine(pos0, pos1, h, info, gf.reshape(1, d), ys, out_rows=out_rows, tm=256,
                    final_norm=final_norm)


def _strict_upper_ones(n):
    r = lax.broadcasted_iota(jnp.int32, (n, n), 0)
    c = lax.broadcasted_iota(jnp.int32, (n, n), 1)
    return (c > r).astype(bf16)


def kernel(x, meta_tokens, norm_mix_g, norm_ffn_g, conv_w_in, conv_w, conv_w_out, attn_w_qkv, attn_w_out, router_group_w, router_group_b, router_expert_w, router_expert_b, moe_w_gate, moe_w_up, moe_w_down, final_norm_g):
    batch, seq_len, d = x.shape
    n_meta = meta_tokens.shape[0]
    n_pad = META_BLOCK - n_meta
    n_tok = batch * seq_len
    assert seq_len % TAIL_ROWS == 0 and d % (2 * N_HEADS) == 0

    x2d = x.reshape(n_tok, d)
    tail = jnp.concatenate([jnp.zeros((n_pad, d), f32), meta_tokens.astype(f32),
                            jnp.zeros((TAIL_ROWS - META_BLOCK, d), f32)], axis=0)

    tc = _largest_divisor(d, (512, 256, 128))
    gated = _conv_in(x2d, tail, norm_mix_g[0].reshape(1, d), conv_w_in[0].astype(bf16), conv_w[0],
                     seq_len=seq_len, tm=TAIL_ROWS, tc=tc)
    h = _matmul_residual(gated, conv_w_out[0].astype(bf16), x2d, tail, tm=TAIL_ROWS)
    h = _moe_layer(h, norm_ffn_g[0], router_group_w[0], router_group_b[0], router_expert_w[0],
                   router_expert_b[0], moe_w_gate, moe_w_up, moe_w_down, final_norm_g,
                   layer=0, out_rows=h.shape[0], final_norm=False)

    qkv = _norm_matmul(h, norm_mix_g[1].reshape(1, d), attn_w_qkv[0].astype(bf16),
                       tm=_largest_divisor(h.shape[0], (768, 512)),
                       tn=_largest_divisor(3 * d, (1024, 768, 384)))
    o = _attention(qkv, _strict_upper_ones(ATT_BLOCK), batch=batch, seq_len=seq_len, d_model=d,
                   n_pad=n_pad)
    h = _matmul_residual(o, attn_w_out[0].astype(bf16), h, tm=TAIL_ROWS)
    out = _moe_layer(h, norm_ffn_g[1], router_group_w[1], router_group_b[1], router_expert_w[1],
                     router_expert_b[1], moe_w_gate, moe_w_up, moe_w_down, final_norm_g,
                     layer=1, out_rows=n_tok, final_norm=True)
    return out.reshape(batch, seq_len, d)
```

```python
import functools

import jax
import jax.numpy as jnp
from jax import lax
from jax.experimental import pallas as pl
from jax.experimental.pallas import tpu as pltpu

N_HEADS = 16
N_GROUPS = 4
TOP_K = 2
CONV_WIDTH = 3
EPS = 1e-6

LANES = 128
SUBLANES = 8
LOG2_SUBLANES = SUBLANES.bit_length() - 1
META_BLOCK = 128
TAIL_ROWS = 512
MOE_TILE = 256
ATT_BLOCK = 256
ATT_HEADS = 2
ATT_QSUB = 2
INFO_ROWS = 8
SKIP_BELOW = -104.0
LOG2E = 1.4426950408889634
VMEM_LIMIT = 56 * 1024 * 1024

f32 = jnp.float32
bf16 = jnp.bfloat16


def _cparams(sem):
    return pltpu.CompilerParams(dimension_semantics=sem, vmem_limit_bytes=VMEM_LIMIT)


def _largest_divisor(n, candidates):
    return next(c for c in candidates if n % c == 0)


def _rms_scale(x):
    return x * lax.rsqrt(jnp.mean(x * x, axis=-1, keepdims=True) + EPS)


def _norm_matmul_kernel(h_ref, g_ref, w_ref, o_ref, xn_ref):
    @pl.when(pl.program_id(1) == 0)
    def _():
        xn_ref[...] = (_rms_scale(h_ref[...]) * g_ref[...]).astype(bf16)

    o_ref[...] = jnp.dot(xn_ref[...], w_ref[...], preferred_element_type=f32).astype(o_ref.dtype)


def _norm_matmul(h, g, w, *, tm, tn):
    rows, d = h.shape
    n = w.shape[1]
    return pl.pallas_call(
        _norm_matmul_kernel,
        grid=(rows // tm, n // tn),
        in_specs=[
            pl.BlockSpec((tm, d), lambda i, j: (i, 0)),
            pl.BlockSpec((1, d), lambda i, j: (0, 0)),
            pl.BlockSpec((d, tn), lambda i, j: (0, j)),
        ],
        out_specs=pl.BlockSpec((tm, tn), lambda i, j: (i, j)),
        out_shape=jax.ShapeDtypeStruct((rows, n), bf16),
        scratch_shapes=[pltpu.VMEM((tm, d), bf16)],
        compiler_params=_cparams(("arbitrary", "arbitrary")),
        name="norm_qkv",
    )(h, g, w)


def _conv_in_kernel(x_ref, g_ref, wb_ref, wc_ref, wh_ref, cw_ref, lead_ref, o_ref, meta_v_ref,
                    xn_ref, carry_ref, *, tiles_per_seq):
    i = pl.program_id(0)
    j = pl.program_id(1)
    tm, tc = o_ref.shape

    @pl.when(j == 0)
    def _():
        xn_ref[...] = (_rms_scale(x_ref[...]) * g_ref[...]).astype(bf16)

    xn = xn_ref[...]
    b_gate = jnp.dot(xn, wb_ref[...], preferred_element_type=f32)
    v = (jnp.dot(xn, wc_ref[...], preferred_element_type=f32)
         * jnp.dot(xn, wh_ref[...], preferred_element_type=f32))

    seq_start = lax.rem(i, tiles_per_seq) == 0
    prev = jnp.where(seq_start, lead_ref[...], carry_ref[j])
    p1 = prev[SUBLANES - 1:SUBLANES, :]
    p2 = prev[SUBLANES - 2:SUBLANES - 1, :]
    row = lax.broadcasted_iota(jnp.int32, (tm, tc), 0)
    s1 = jnp.where(row == 0, p1, pltpu.roll(v, 1, 0))
    s2 = jnp.where(row == 0, p2, jnp.where(row == 1, p1, pltpu.roll(v, 2, 0)))
    y = cw_ref[0:1, :] * s2 + cw_ref[1:2, :] * s1 + cw_ref[2:3, :] * v
    o_ref[...] = (b_gate * y).astype(o_ref.dtype)

    carry_ref[j] = v[tm - SUBLANES:, :]

    meta_v_ref[...] = v[META_BLOCK - SUBLANES:META_BLOCK, :]


def _conv_in(x2d, g, w_in, conv_w, lead, *, seq_len, tm, tc):
    rows, d = x2d.shape
    n_j = d // tc
    assert seq_len % tm == 0 and rows % tm == 0
    kern = functools.partial(_conv_in_kernel, tiles_per_seq=seq_len // tm)
    return pl.pallas_call(
        kern,
        grid=(rows // tm, n_j),
        in_specs=[
            pl.BlockSpec((tm, d), lambda i, j: (i, 0)),
            pl.BlockSpec((1, d), lambda i, j: (0, 0)),
            pl.BlockSpec((d, tc), lambda i, j: (0, j)),
            pl.BlockSpec((d, tc), lambda i, j: (0, n_j + j)),
            pl.BlockSpec((d, tc), lambda i, j: (0, 2 * n_j + j)),
            pl.BlockSpec((CONV_WIDTH, tc), lambda i, j: (0, j)),
            pl.BlockSpec((SUBLANES, tc), lambda i, j: (0, j)),
        ],
        out_specs=[pl.BlockSpec((tm, tc), lambda i, j: (i, j)),
                   pl.BlockSpec((SUBLANES, tc), lambda i, j: (i, j))],
        out_shape=[jax.ShapeDtypeStruct((rows, d), bf16),
                   jax.ShapeDtypeStruct((SUBLANES * (rows // tm), d), f32)],
        scratch_shapes=[pltpu.VMEM((tm, d), bf16), pltpu.VMEM((n_j, SUBLANES, tc), f32)],
        compiler_params=_cparams(("arbitrary", "arbitrary")),
        name="conv_in",
    )(x2d, g, w_in, w_in, w_in, conv_w, lead)


def _matmul_residual_kernel(a_ref, w_ref, r_ref, o_ref):
    o_ref[...] = r_ref[...] + jnp.dot(a_ref[...], w_ref[...], preferred_element_type=f32)


def _matmul_residual_split_kernel(a_ref, at_ref, w_ref, r_ref, rt_ref, o_ref, *, n_main):
    @pl.when(pl.program_id(0) < n_main)
    def _():
        o_ref[...] = r_ref[...] + jnp.dot(a_ref[...], w_ref[...], preferred_element_type=f32)

    @pl.when(pl.program_id(0) >= n_main)
    def _():
        o_ref[...] = rt_ref[...] + jnp.dot(at_ref[...], w_ref[...], preferred_element_type=f32)


def _matmul_residual(a, w, r, *, tm):
    rows, k = a.shape
    n = w.shape[1]
    return pl.pallas_call(
        _matmul_residual_kernel,
        grid=(rows // tm,),
        in_specs=[pl.BlockSpec((tm, k), lambda i: (i, 0)),
                  pl.BlockSpec((k, n), lambda i: (0, 0)),
                  pl.BlockSpec((tm, n), lambda i: (i, 0))],
        out_specs=pl.BlockSpec((tm, n), lambda i: (i, 0)),
        out_shape=jax.ShapeDtypeStruct((rows, n), f32),
        compiler_params=_cparams(("arbitrary",)),
        name="out_proj_residual",
    )(a, w, r)


def _matmul_residual_split(a, a_tail, w, r, r_tail, *, tm):
    rows, k = a.shape
    n = w.shape[1]
    n_main = rows // tm
    assert rows % tm == 0 and a_tail.shape[0] == tm and r.shape[0] == rows and r_tail.shape[0] == tm
    kern = functools.partial(_matmul_residual_split_kernel, n_main=n_main)

    def main_blk(i):
        return (jnp.minimum(i, n_main - 1), 0)

    return pl.pallas_call(
        kern,
        grid=(n_main + 1,),
        in_specs=[pl.BlockSpec((tm, k), main_blk),
                  pl.BlockSpec((tm, k), lambda i: (0, 0)),
                  pl.BlockSpec((k, n), lambda i: (0, 0)),
                  pl.BlockSpec((tm, n), main_blk),
                  pl.BlockSpec((tm, n), lambda i: (0, 0))],
        out_specs=pl.BlockSpec((tm, n), lambda i: (i, 0)),
        out_shape=jax.ShapeDtypeStruct((rows + tm, n), f32),
        compiler_params=_cparams(("arbitrary",)),
        name="out_proj_residual",
    )(a, a_tail, w, r, r_tail)


def _sb_scores(q, kb, scale):
    return lax.dot_general(kb, q, (((1,), (1,)), ((), ())), preferred_element_type=f32) * scale


def _sb_logs(z, mask):
    softplus_neg = jnp.log(1.0 + jnp.exp2(jnp.abs(z) * (-LOG2E)))
    log_sig = jnp.minimum(z, 0.0) - softplus_neg
    log_1m = log_sig - z
    if mask is not None:
        log_1m = jnp.where(mask, log_1m, 0.0)
    hi = log_1m.astype(bf16)
    lo = (log_1m - hi.astype(f32)).astype(bf16)
    return log_sig, log_1m[0:1, :], jnp.concatenate([hi, lo], axis=1)


def _sb_suffix_sums(u, hilo):
    both = jnp.dot(u, hilo, preferred_element_type=f32)
    tq = hilo.shape[1] // 2
    return both[:, :tq] + both[:, tq:]


def _sb_weights(log_sig, log_1m_row0, rest, carry, mask):
    a = jnp.exp2((log_sig + rest + carry) * LOG2E)
    if mask is not None:
        a = jnp.where(mask, a, 0.0)
    return a.astype(bf16), carry + rest[0:1, :] + log_1m_row0


def _sb_values(vb, a):
    return lax.dot_general(vb, a, (((0,), (0,)), ((), ())), preferred_element_type=f32)


def _sb_block(q, kb, vb, u, carry, mask, scale):
    log_sig, row0, hilo = _sb_logs(_sb_scores(q, kb, scale), mask)
    a, carry = _sb_weights(log_sig, row0, _sb_suffix_sums(u, hilo), carry, mask)
    return _sb_values(vb, a), carry


def _attn_kernel(q_ref, k_ref, v_ref, km_ref, vm_ref, u_ref, o_ref, acc_ref, carry_ref, *,
                 scale, n_pad, n_main_steps):
    step = pl.program_id(1)
    steps_per_seq = k_ref.shape[0] // (ATT_QSUB * ATT_BLOCK)
    qstep = lax.rem(step, steps_per_seq)
    tb = ATT_BLOCK
    dh = q_ref.shape[1] // ATT_HEADS
    u = u_ref[...]
    s_loc = lax.broadcasted_iota(jnp.int32, (tb, tb), 0)
    t_loc = lax.broadcasted_iota(jnp.int32, (tb, tb), 1)
    causal = s_loc < t_loc
    meta_keys = (s_loc >= n_pad) & (s_loc < META_BLOCK)

    @pl.when(step < n_main_steps)
    def _():
        chains = [(hh, qs) for hh in range(ATT_HEADS) for qs in range(ATT_QSUB)]
        first = qstep == 0

        def keys(ref, mref, hh, qs, prev):
            cols = slice(hh * dh, (hh + 1) * dh)
            i = qstep * ATT_QSUB + qs
            blk = jnp.maximum(i - 1, 0) if prev else i
            kv = ref[pl.ds(pl.multiple_of(blk * tb, tb), tb), cols]
            if prev and qs == 0:
                kv = jnp.where(first, mref[:, cols], kv)
            return kv

        items = [(hh, qs, False) for hh, qs in chains] + [(hh, qs, True) for hh, qs in chains]
        n_items = len(items)
        masks = [(meta_keys | jnp.logical_not(first)) if (prev and qs == 0)
                 else (None if prev else causal) for hh, qs, prev in items]
        z, logs, rest, wts, contrib, carry_out = {}, {}, {}, {}, {}, {}
        for t in range(n_items + 4):
            if 0 <= t - 4 < n_items:
                n = t - 4
                hh, qs, prev = items[n]
                contrib[n] = _sb_values(keys(v_ref, vm_ref, hh, qs, prev), wts.pop(n))
            if 0 <= t - 3 < n_items:
                n = t - 3
                hh, qs, prev = items[n]
                carry_in = carry_out[n - len(chains)] if prev else jnp.zeros((1, tb), f32)
                log_sig, row0, _ = logs.pop(n)
                wts[n], carry_out[n] = _sb_weights(log_sig, row0, rest.pop(n), carry_in, masks[n])
            if 0 <= t - 2 < n_items:
                rest[t - 2] = _sb_suffix_sums(u, logs[t - 2][2])
            if 0 <= t - 1 < n_items:
                logs[t - 1] = _sb_logs(z.pop(t - 1), masks[t - 1])
            if t < n_items:
                hh, qs, prev = items[t]
                q = q_ref[qs * tb:(qs + 1) * tb, hh * dh:(hh + 1) * dh]
                z[t] = _sb_scores(q, keys(k_ref, km_ref, hh, qs, prev), scale)
        for c, (hh, qs) in enumerate(chains):
            acc_ref[hh, qs] = contrib[c] + contrib[c + len(chains)]
            carry_ref[hh, qs] = carry_out[c + len(chains)]

        top_all = jnp.max(functools.reduce(
            jnp.maximum, [carry_out[c + len(chains)] for c in range(len(chains))]))

        @pl.when(top_all > SKIP_BELOW)
        def _():
            for hh, qs in chains:
                cols = slice(hh * dh, (hh + 1) * dh)
                i = qstep * ATT_QSUB + qs

                def cond(state):
                    j, top = state
                    return (j >= 0) & (top > SKIP_BELOW)

                def body(state, hh=hh, qs=qs, cols=cols):
                    j, _ = state
                    st = pl.multiple_of(j * tb, tb)
                    q = q_ref[qs * tb:(qs + 1) * tb, cols]
                    contrib, carry = _sb_block(q, k_ref[pl.ds(st, tb), cols],
                                               v_ref[pl.ds(st, tb), cols], u, carry_ref[hh, qs],
                                               None, scale)
                    acc_ref[hh, qs] += contrib
                    carry_ref[hh, qs] = carry
                    return j - 1, jnp.max(carry)

                _, top = lax.while_loop(cond, body, (i - 2, jnp.max(carry_ref[hh, qs])))

                @pl.when((top > SKIP_BELOW) & (i >= 1))
                def _(hh=hh, qs=qs, cols=cols):
                    q = q_ref[qs * tb:(qs + 1) * tb, cols]
                    contrib, _ = _sb_block(q, km_ref[:, cols], vm_ref[:, cols], u,
                                           carry_ref[hh, qs], meta_keys, scale)
                    acc_ref[hh, qs] += contrib

        for hh, qs in chains:
            o_ref[qs * tb:(qs + 1) * tb, hh * dh:(hh + 1) * dh] = acc_ref[hh, qs].T.astype(o_ref.dtype)

    @pl.when(step >= n_main_steps)
    def _():
        o_ref[...] = jnp.zeros(o_ref.shape, o_ref.dtype)
        for hh in range(ATT_HEADS):
            cols = slice(hh * dh, (hh + 1) * dh)
            contrib, _ = _sb_block(q_ref[0:tb, cols], km_ref[:, cols], vm_ref[:, cols], u,
                                   jnp.zeros((1, tb), f32), causal & meta_keys, scale)
            o_ref[0:tb, cols] = contrib.T.astype(o_ref.dtype)


def _attention(qkv, u, *, batch, seq_len, d_model, n_pad):
    rows = qkv.shape[0]
    dh = d_model // N_HEADS
    tqs = ATT_QSUB * ATT_BLOCK
    assert tqs == TAIL_ROWS and seq_len % tqs == 0 and N_HEADS % ATT_HEADS == 0
    steps_per_seq = seq_len // tqs
    n_main = batch * steps_per_seq
    n_hp = N_HEADS // ATT_HEADS
    wcols = ATT_HEADS * dh
    meta_blk = (batch * seq_len) // ATT_BLOCK
    kern = functools.partial(_attn_kernel, scale=dh ** -0.5, n_pad=n_pad, n_main_steps=n_main)

    def seq_of(s):
        return jnp.minimum(s // steps_per_seq, batch - 1)

    return pl.pallas_call(
        kern,
        grid=(n_hp, n_main + 1),
        in_specs=[
            pl.BlockSpec((tqs, wcols), lambda h, s: (s, h)),
            pl.BlockSpec((seq_len, wcols), lambda h, s: (seq_of(s), n_hp + h)),
            pl.BlockSpec((seq_len, wcols), lambda h, s: (seq_of(s), 2 * n_hp + h)),
            pl.BlockSpec((ATT_BLOCK, wcols), lambda h, s: (meta_blk, n_hp + h)),
            pl.BlockSpec((ATT_BLOCK, wcols), lambda h, s: (meta_blk, 2 * n_hp + h)),
            pl.BlockSpec((ATT_BLOCK, ATT_BLOCK), lambda h, s: (0, 0)),
        ],
        out_specs=pl.BlockSpec((tqs, wcols), lambda h, s: (s, h)),
        out_shape=jax.ShapeDtypeStruct((rows, d_model), bf16),
        scratch_shapes=[pltpu.VMEM((ATT_HEADS, ATT_QSUB, dh, ATT_BLOCK), f32),
                        pltpu.VMEM((ATT_HEADS, ATT_QSUB, 1, ATT_BLOCK), f32)],
        compiler_params=_cparams(("arbitrary", "arbitrary")),
        name="sb_attention",
    )(qkv, qkv, qkv, qkv, qkv, u)


def _pack_rows(x, w_ref, z_ref):
    m, d = x.shape
    half = d // 2
    for c in range(half // LANES):
        z_ref[pl.ds(0, m, stride=2), :] = x[:, c * LANES:(c + 1) * LANES]
        z_ref[pl.ds(1, m, stride=2), :] = x[:, half + c * LANES:half + (c + 1) * LANES]
        words = pltpu.bitcast(z_ref[...].astype(bf16), jnp.uint32)
        w_ref[:, c, :, :] = words.reshape(m // SUBLANES, SUBLANES, LANES)


def _unpack_rows(w_ref, z_ref):
    m = w_ref.shape[0] * SUBLANES
    lo, hi = [], []
    for c in range(w_ref.shape[1]):
        words = w_ref[:, c, :, :].reshape(m, LANES)
        z_ref[...] = pltpu.bitcast(words, bf16).astype(f32)
        lo.append(z_ref[pl.ds(0, m, stride=2), :].astype(bf16))
        hi.append(z_ref[pl.ds(1, m, stride=2), :].astype(bf16))
    return jnp.concatenate(lo + hi, axis=1)


def _row_tiled(rows, width):
    return (rows // SUBLANES, width // LANES, SUBLANES, LANES)


def _row_of(ref, r):
    return ref.at[lax.shift_right_logical(r, LOG2_SUBLANES), :, pl.ds(r & (SUBLANES - 1), 1), :]


def _router_kernel(h_ref, g_ref, w_ref, b_ref, xp_ref, info_ref, infot_ref, cnt_ref, base_ref,
                   z_ref, *, n_experts):
    i = pl.program_id(0)
    tm = h_ref.shape[0]
    epg = n_experts // N_GROUPS

    @pl.when(i == 0)
    def _():
        base_ref[...] = jnp.zeros(base_ref.shape, f32)

    xn = _rms_scale(h_ref[...]) * g_ref[...]
    _pack_rows(xn, xp_ref, z_ref)

    xh = xn.astype(bf16)
    xl = (xn - xh.astype(f32)).astype(bf16)
    hh_hl = jnp.dot(xh, w_ref[...], preferred_element_type=f32)
    logits = (hh_hl[:, :LANES] + hh_hl[:, LANES:]
              + jnp.dot(xl, w_ref[:, :LANES], preferred_element_type=f32) + b_ref[...])
    lane = lax.broadcasted_iota(jnp.int32, logits.shape, 1)
    neg = jnp.float32(-jnp.inf)
    big = jnp.int32(LANES)

    gl = jnp.where(lane < N_GROUPS, logits, neg)
    gmax = jnp.max(gl, axis=-1, keepdims=True)
    g_sel = jnp.min(jnp.where(gl == gmax, lane, big), axis=-1, keepdims=True)
    p_top = 1.0 / jnp.sum(jnp.exp(gl - gmax), axis=-1, keepdims=True)

    lo_lane = N_GROUPS + g_sel * epg
    el = jnp.where((lane >= lo_lane) & (lane < lo_lane + epg), logits, neg)
    v1 = jnp.max(el, axis=-1, keepdims=True)
    l1 = jnp.min(jnp.where(el == v1, lane, big), axis=-1, keepdims=True)
    el2 = jnp.where(lane == l1, neg, el)
    v2 = jnp.max(el2, axis=-1, keepdims=True)
    l2 = jnp.min(jnp.where(el2 == v2, lane, big), axis=-1, keepdims=True)
    r = jnp.exp(v2 - v1)
    gate1 = p_top * (1.0 / (1.0 + r))
    gate2 = p_top * (r / (1.0 + r))

    onehot = ((lane == l1) | (lane == l2)).astype(f32)
    r_i = lax.broadcasted_iota(jnp.int32, (tm, tm), 0)
    c_i = lax.broadcasted_iota(jnp.int32, (tm, tm), 1)
    earlier = jnp.where(c_i < r_i, 1.0, 0.0).astype(bf16)
    before = jnp.dot(earlier, onehot.astype(bf16), preferred_element_type=f32) + base_ref[...]
    rank1 = jnp.sum(jnp.where(lane == l1, before, 0.0), axis=-1, keepdims=True)
    rank2 = jnp.sum(jnp.where(lane == l2, before, 0.0), axis=-1, keepdims=True)
    base_ref[...] = base_ref[...] + jnp.sum(onehot, axis=0, keepdims=True)
    cnt_ref[...] = base_ref[...].astype(jnp.int32)

    e1 = jnp.clip(l1 - N_GROUPS, 0, n_experts - 1)
    e2 = jnp.clip(l2 - N_GROUPS, 0, n_experts - 1)
    info = jnp.where(lane == 0, e1, 0)
    info = jnp.where(lane == 1, e2, info)
    info = jnp.where(lane == 2, rank1.astype(jnp.int32), info)
    info = jnp.where(lane == 3, rank2.astype(jnp.int32), info)
    info = jnp.where(lane == 4, pltpu.bitcast(jnp.broadcast_to(gate1, logits.shape), jnp.int32), info)
    info = jnp.where(lane == 5, pltpu.bitcast(jnp.broadcast_to(gate2, logits.shape), jnp.int32), info)
    info_ref[...] = info
    infot_ref[...] = info.T[0:INFO_ROWS, :]


def _router(h, g, w_r, b_r, *, n_experts, tm):
    rows, d = h.shape
    kern = functools.partial(_router_kernel, n_experts=n_experts)
    return pl.pallas_call(
        kern,
        grid=(rows // tm,),
        in_specs=[
            pl.BlockSpec((tm, d), lambda i: (i, 0)),
            pl.BlockSpec((1, d), lambda i: (0, 0)),
            pl.BlockSpec((d, 2 * LANES), lambda i: (0, 0)),
            pl.BlockSpec((1, LANES), lambda i: (0, 0)),
        ],
        out_specs=[
            pl.BlockSpec(_row_tiled(tm, d // 2), lambda i: (i, 0, 0, 0)),
            pl.BlockSpec((tm, LANES), lambda i: (i, 0)),
            pl.BlockSpec((INFO_ROWS, tm), lambda i: (0, i)),
            pl.BlockSpec((1, LANES), lambda i: (0, 0)),
        ],
        out_shape=[
            jax.ShapeDtypeStruct(_row_tiled(rows, d // 2), jnp.uint32),
            jax.ShapeDtypeStruct((rows, LANES), jnp.int32),
            jax.ShapeDtypeStruct((INFO_ROWS, rows), jnp.int32),
            jax.ShapeDtypeStruct((1, LANES), jnp.int32),
        ],
        scratch_shapes=[pltpu.VMEM((1, LANES), f32), pltpu.VMEM((2 * tm, LANES), f32)],
        compiler_params=_cparams(("arbitrary",)),
        name="moe_router",
    )(h, g, w_r, b_r)


def _pad_fill_copies(fs_ref, pad_ref, zbuf, xs_ref, sem, e):
    fs = fs_ref[e]
    pad = pad_ref[e]
    head = pad & (SUBLANES - 1)
    out = []
    for k in range(SUBLANES - 1):
        out.append((k < head,
                    pltpu.make_async_copy(zbuf.at[0, :, pl.ds(0, 1), :], _row_of(xs_ref, fs + k), sem)))
    cur = lax.shift_right_logical(fs + head, LOG2_SUBLANES)
    for b in reversed(range(SUBLANES.bit_length() - 1, MOE_TILE.bit_length() - 1)):
        size = 1 << b
        groups = size // SUBLANES
        out.append(((pad & size) != 0,
                    pltpu.make_async_copy(zbuf.at[pl.ds(0, groups)], xs_ref.at[pl.ds(cur, groups)], sem)))
        cur = cur + lax.shift_right_logical(pad & size, LOG2_SUBLANES)
    return out


def _unused_tile_copies(zbuf, xs_ref, sem, t):
    groups = zbuf.shape[0]
    per_tile = MOE_TILE // SUBLANES
    return [pltpu.make_async_copy(zbuf, xs_ref.at[pl.ds(t * per_tile + k * groups, groups)], sem)
            for k in range(per_tile // groups)]


def _dispatch_kernel(p0_ref, p1_ref, fs_ref, pad_ref, nu_ref, x_ref, xs_ref, zbuf, sem, zsem, *,
                     n_steps, n_experts, max_tiles):
    i = pl.program_id(0)
    tm = x_ref.shape[0] * SUBLANES
    base = i * tm

    @pl.when(i == 0)
    def _():
        zbuf[...] = jnp.zeros(zbuf.shape, zbuf.dtype)

        def fill(e, c):
            for needed, copy in _pad_fill_copies(fs_ref, pad_ref, zbuf, xs_ref, zsem, e):
                @pl.when(needed)
                def _():
                    copy.start()
            return c

        lax.fori_loop(0, n_experts, fill, 0)

        def fill_unused(t, c):
            for copy in _unused_tile_copies(zbuf, xs_ref, zsem, t):
                copy.start()
            return c

        lax.fori_loop(nu_ref[0], max_tiles, fill_unused, 0)

    def issue(g, c):
        for k in range(SUBLANES):
            src = x_ref.at[g, :, pl.ds(k, 1), :]
            r = base + g * SUBLANES + k
            pltpu.make_async_copy(src, _row_of(xs_ref, p0_ref[r]), sem).start(priority=0)
            pltpu.make_async_copy(src, _row_of(xs_ref, p1_ref[r]), sem).start(priority=1)
        return c

    lax.fori_loop(0, tm // SUBLANES, issue, 0)
    for _ in range(TOP_K):
        pltpu.make_async_copy(x_ref, xs_ref.at[pl.ds(0, tm // SUBLANES)], sem).wait()

    @pl.when(i == n_steps - 1)
    def _():
        def drain(e, c):
            for needed, copy in _pad_fill_copies(fs_ref, pad_ref, zbuf, xs_ref, zsem, e):
                @pl.when(needed)
                def _():
                    copy.wait()
            return c

        lax.fori_loop(0, n_experts, drain, 0)

        def drain_unused(t, c):
            for copy in _unused_tile_copies(zbuf, xs_ref, zsem, t):
                copy.wait()
            return c

        lax.fori_loop(nu_ref[0], max_tiles, drain_unused, 0)


def _dispatch(pos0, pos1, fill_start, pad, n_used, xp, *, sorted_rows, tm):
    width = xp.shape[1] * LANES
    n_steps = (xp.shape[0] * SUBLANES) // tm
    kern = functools.partial(_dispatch_kernel, n_steps=n_steps, n_experts=pad.shape[0],
                             max_tiles=sorted_rows // MOE_TILE)
    return pl.pallas_call(
        kern,
        grid_spec=pltpu.PrefetchScalarGridSpec(
            num_scalar_prefetch=5,
            grid=(n_steps,),
            in_specs=[pl.BlockSpec(_row_tiled(tm, width), lambda i, *_: (i, 0, 0, 0))],
            out_specs=pl.BlockSpec(memory_space=pl.ANY),
            scratch_shapes=[pltpu.VMEM(_row_tiled(MOE_TILE // 2, width), jnp.uint32),
                            pltpu.SemaphoreType.DMA(()), pltpu.SemaphoreType.DMA(())],
        ),
        out_shape=jax.ShapeDtypeStruct(_row_tiled(sorted_rows, width), jnp.uint32),
        compiler_params=pltpu.CompilerParams(dimension_semantics=("arbitrary",),
                                             vmem_limit_bytes=VMEM_LIMIT, has_side_effects=True),
        name="moe_dispatch",
    )(pos0, pos1, fill_start, pad, n_used, xp)


def _experts_kernel(te_ref, tf_ref, nx_ref, nu_ref, xs_ref, wg_hbm, wu_hbm, wd_hbm, y_ref,
                    wg_f, wu_f, wd_f, wg_b, wu_b, wd_b, z_ref, slot_ref, sems, *, layer):
    t = pl.program_id(0)

    def weight_copies(e, slot):
        return [pltpu.make_async_copy(hbm.at[layer, e], buf.at[slot], sems.at[slot, k])
                for k, (hbm, buf) in enumerate(((wg_hbm, wg_f), (wu_hbm, wu_f), (wd_hbm, wd_f)))]

    @pl.when(t == 0)
    def _():
        slot_ref[0] = 0
        for c in weight_copies(te_ref[0], 0):
            c.start()

    @pl.when(tf_ref[t] == 1)
    def _():
        slot = slot_ref[0]
        for c in weight_copies(te_ref[t], slot):
            c.wait()

        @pl.when(nx_ref[t] >= 0)
        def _():
            for c in weight_copies(nx_ref[t], 1 - slot):
                c.start()

        wg_b[...] = wg_f[slot].astype(bf16)
        wu_b[...] = wu_f[slot].astype(bf16)
        wd_b[...] = wd_f[slot].astype(bf16)
        slot_ref[0] = 1 - slot

    @pl.when(t < nu_ref[0])
    def _():
        x = _unpack_rows(xs_ref, z_ref)
        gate = jnp.dot(x, wg_b[...], preferred_element_type=f32)
        up = jnp.dot(x, wu_b[...], preferred_element_type=f32)
        hidden = (gate * (1.0 / (1.0 + jnp.exp(-gate))) * up).astype(bf16)
        y = jnp.dot(hidden, wd_b[...], preferred_element_type=f32)
        for c in range(y_ref.shape[1]):
            y_ref[:, c, :, :] = y[:, c * LANES:(c + 1) * LANES].reshape(y_ref.shape[0], SUBLANES, LANES)

    @pl.when(t >= nu_ref[0])
    def _():
        y_ref[...] = jnp.zeros(y_ref.shape, y_ref.dtype)


def _experts(tile_expert, tile_first, next_expert, n_used, xs, w_gate, w_up, w_down, *, layer,
             max_tiles):
    tm = MOE_TILE
    _, _, d, f = w_gate.shape
    kern = functools.partial(_experts_kernel, layer=layer)
    hbm = pl.BlockSpec(memory_space=pl.ANY)
    return pl.pallas_call(
        kern,
        grid_spec=pltpu.PrefetchScalarGridSpec(
            num_scalar_prefetch=4,
            grid=(max_tiles,),
            in_specs=[pl.BlockSpec(_row_tiled(tm, d // 2),
                                   lambda t, te, tf, nx, nu: (jnp.clip(nu[0] - 1, 0, t), 0, 0, 0)),
                      hbm, hbm, hbm],
            out_specs=pl.BlockSpec(_row_tiled(tm, d), lambda t, *_: (t, 0, 0, 0)),
            scratch_shapes=[pltpu.VMEM((2, d, f), f32), pltpu.VMEM((2, d, f), f32),
                            pltpu.VMEM((2, f, d), f32),
                            pltpu.VMEM((d, f), bf16), pltpu.VMEM((d, f), bf16),
                            pltpu.VMEM((f, d), bf16), pltpu.VMEM((2 * tm, LANES), f32),
                            pltpu.SMEM((1,), jnp.int32), pltpu.SemaphoreType.DMA((2, 3))],
        ),
        out_shape=jax.ShapeDtypeStruct(_row_tiled(max_tiles * tm, d), f32),
        compiler_params=_cparams(("arbitrary",)),
        name="moe_experts",
    )(tile_expert, tile_first, next_expert, n_used, xs, w_gate, w_up, w_down)


def _combine_kernel(p0_ref, p1_ref, h_ref, info_ref, gf_ref, ys_ref, o_ref, buf, sems, *,
                    n_steps, final_norm):
    i = pl.program_id(0)
    tm = h_ref.shape[0]

    def issue(step, slot):
        base = step * tm

        def body(g, c):
            for k in range(SUBLANES):
                r = base + g * SUBLANES + k
                pltpu.make_async_copy(_row_of(ys_ref, p0_ref[r]),
                                      buf.at[slot, 0, g, :, pl.ds(k, 1), :],
                                      sems.at[slot]).start(priority=0)
                pltpu.make_async_copy(_row_of(ys_ref, p1_ref[r]),
                                      buf.at[slot, 1, g, :, pl.ds(k, 1), :],
                                      sems.at[slot]).start(priority=1)
            return c

        lax.fori_loop(0, tm // SUBLANES, body, 0)

    @pl.when(i == 0)
    def _():
        issue(0, 0)

    @pl.when(i + 1 < n_steps)
    def _():
        issue(i + 1, lax.rem(i + 1, 2))

    slot = lax.rem(i, 2)
    for k in range(TOP_K):
        pltpu.make_async_copy(ys_ref.at[pl.ds(0, tm // SUBLANES)], buf.at[slot, k],
                              sems.at[slot]).wait()

    info = info_ref[...]
    lane = lax.broadcasted_iota(jnp.int32, info.shape, 1)
    gates = pltpu.bitcast(info, f32)
    g1 = jnp.sum(jnp.where(lane == 4, gates, 0.0), axis=-1, keepdims=True)
    g2 = jnp.sum(jnp.where(lane == 5, gates, 0.0), axis=-1, keepdims=True)
    cols = [g1 * buf[slot, 0, :, c, :, :].reshape(tm, LANES)
            + g2 * buf[slot, 1, :, c, :, :].reshape(tm, LANES) for c in range(buf.shape[3])]
    out = h_ref[...] + jnp.concatenate(cols, axis=1)
    if final_norm:
        out = _rms_scale(out) * gf_ref[...]
    o_ref[...] = out


def _combine(pos0, pos1, h, info, gf, ys, *, out_rows, tm, final_norm):
    d = h.shape[1]
    n_steps = out_rows // tm
    kern = functools.partial(_combine_kernel, n_steps=n_steps, final_norm=final_norm)
    return pl.pallas_call(
        kern,
        grid_spec=pltpu.PrefetchScalarGridSpec(
            num_scalar_prefetch=2,
            grid=(n_steps,),
            in_specs=[
                pl.BlockSpec((tm, d), lambda i, p0, p1: (i, 0)),
                pl.BlockSpec((tm, LANES), lambda i, p0, p1: (i, 0)),
                pl.BlockSpec((1, d), lambda i, p0, p1: (0, 0)),
                pl.BlockSpec(memory_space=pl.ANY),
            ],
            out_specs=pl.BlockSpec((tm, d), lambda i, p0, p1: (i, 0)),
            scratch_shapes=[pltpu.VMEM((2, TOP_K) + _row_tiled(tm, d), f32),
                            pltpu.SemaphoreType.DMA((2,))],
        ),
        out_shape=jax.ShapeDtypeStruct((out_rows, d), f32),
        compiler_params=_cparams(("arbitrary",)),
        name="moe_combine",
    )(pos0, pos1, h, info, gf, ys)


def _table_lookup(table, idx):
    k = lax.broadcasted_iota(jnp.int32, (table.shape[0], idx.shape[0]), 0)
    return jnp.sum(jnp.where(k == idx[None, :], table[:, None], 0), axis=0)


def _moe_layer(h, g_ffn, wg_r, bg_r, we_r, be_r, w_gate, w_up, w_down, gf, *, layer, out_rows,
               final_norm):
    rows, d = h.shape
    n_experts = we_r.shape[1]
    n_fill = LANES - N_GROUPS - n_experts
    w_r = jnp.concatenate([wg_r, we_r, jnp.zeros((d, n_fill), f32)], axis=1)
    b_r = jnp.concatenate([bg_r, be_r, jnp.zeros((n_fill,), f32)]).reshape(1, LANES)
    w_hi = w_r.astype(bf16)
    w_hl = jnp.concatenate([w_hi, (w_r - w_hi.astype(f32)).astype(bf16)], axis=1)

    xp, info, info_t, cnt = _router(h, g_ffn.reshape(1, d), w_hl, b_r, n_experts=n_experts, tm=512)

    tm = MOE_TILE
    max_tiles = (TOP_K * rows) // tm + n_experts
    i32 = jnp.int32
    cnt = cnt[0, N_GROUPS:N_GROUPS + n_experts]
    n_tiles = (cnt + tm - 1) // tm
    tile_end = jnp.cumsum(n_tiles)
    tile_start = tile_end - n_tiles
    n_used = tile_end[-1]
    t_ids = jnp.arange(max_tiles, dtype=i32)
    used = t_ids < n_used
    te = jnp.minimum(jnp.sum(tile_end[None, :] <= t_ids[:, None], axis=1), n_experts - 1)
    tile_first = (used & (t_ids == tile_start[te])).astype(i32)
    nxt_tile = tile_end[te]
    next_expert = jnp.where(nxt_tile < n_used, te[jnp.minimum(nxt_tile, max_tiles - 1)], -1)

    sorted_rows = max_tiles * tm
    row_off = (tile_start * tm).astype(i32)
    pos0 = jnp.clip(_table_lookup(row_off, info_t[0]) + info_t[2], 0, sorted_rows - 1)
    pos1 = jnp.clip(_table_lookup(row_off, info_t[1]) + info_t[3], 0, sorted_rows - 1)

    n_used = n_used.reshape(1).astype(i32)
    xs = _dispatch(pos0, pos1, (row_off + cnt).astype(i32), (n_tiles * tm - cnt).astype(i32), n_used,
                   xp, sorted_rows=sorted_rows, tm=512)
    ys = _experts(te.astype(i32), tile_first, next_expert.astype(i32), n_used,
                  xs, w_gate, w_up, w_down, layer=layer, max_tiles=max_tiles)
    return _combine(pos0, pos1, h, info, gf.reshape(1, d), ys, out_rows=out_rows, tm=256,
                    final_norm=final_norm)


def _strict_upper_ones(n):
    r = lax.broadcasted_iota(jnp.int32, (n, n), 0)
    c = lax.broadcasted_iota(jnp.int32, (n, n), 1)
    return (c > r).astype(bf16)


def kernel(x, meta_tokens, norm_mix_g, norm_ffn_g, conv_w_in, conv_w, conv_w_out, attn_w_qkv, attn_w_out, router_group_w, router_group_b, router_expert_w, router_expert_b, moe_w_gate, moe_w_up, moe_w_down, final_norm_g):
    batch, seq_len, d = x.shape
    n_meta = meta_tokens.shape[0]
    n_pad = META_BLOCK - n_meta
    n_tok = batch * seq_len
    assert seq_len % TAIL_ROWS == 0 and d % (2 * N_HEADS) == 0

    x2d = x.reshape(n_tok, d)
    tail = jnp.concatenate([jnp.zeros((n_pad, d), f32), meta_tokens.astype(f32),
                            jnp.zeros((TAIL_ROWS - META_BLOCK, d), f32)], axis=0)

    tc = _largest_divisor(d, (512, 256, 128))
    g_mix0 = norm_mix_g[0].reshape(1, d)
    w_in = conv_w_in[0].astype(bf16)
    gated_tail, meta_v = _conv_in(tail, g_mix0, w_in, conv_w[0], jnp.zeros((SUBLANES, d), f32),
                                  seq_len=TAIL_ROWS, tm=TAIL_ROWS, tc=tc)
    gated, _ = _conv_in(x2d, g_mix0, w_in, conv_w[0], meta_v, seq_len=seq_len,
                        tm=_largest_divisor(seq_len, (1024, 512)), tc=tc)
    h = _matmul_residual_split(gated, gated_tail, conv_w_out[0].astype(bf16), x2d, tail,
                               tm=TAIL_ROWS)
    h = _moe_layer(h, norm_ffn_g[0], router_group_w[0], router_group_b[0], router_expert_w[0],
                   router_expert_b[0], moe_w_gate, moe_w_up, moe_w_down, final_norm_g,
                   layer=0, out_rows=h.shape[0], final_norm=False)

    qkv = _norm_matmul(h, norm_mix_g[1].reshape(1, d), attn_w_qkv[0].astype(bf16),
                       tm=_largest_divisor(h.shape[0], (1536, 512)),
                       tn=_largest_divisor(3 * d, (512, 384)))
    o = _attention(qkv, _strict_upper_ones(ATT_BLOCK), batch=batch, seq_len=seq_len, d_model=d,
                   n_pad=n_pad)
    h = _matmul_residual(o, attn_w_out[0].astype(bf16), h, tm=TAIL_ROWS)
    out = _moe_layer(h, norm_ffn_g[1], router_group_w[1], router_group_b[1], router_expert_w[1],
                     router_expert_b[1], moe_w_gate, moe_w_up, moe_w_down, final_norm_g,
                     layer=1, out_rows=n_tok, final_norm=True)
    return out.reshape(batch, seq_len, d)
```

```python
import functools

import jax
import jax.numpy as jnp
from jax import lax
from jax.experimental import pallas as pl
from jax.experimental.pallas import tpu as pltpu

N_HEADS = 16
N_GROUPS = 4
TOP_K = 2
CONV_WIDTH = 3
EPS = 1e-6

LANES = 128
SUBLANES = 8
LOG2_SUBLANES = SUBLANES.bit_length() - 1
META_BLOCK = 128
TAIL_ROWS = 512
MOE_TILE = 256
ATT_BLOCK = 256
ATT_HEADS = 4
ATT_QSUB = 2
INFO_ROWS = 8
SKIP_BELOW = -104.0
LOG2E = 1.4426950408889634
VMEM_LIMIT = 56 * 1024 * 1024

f32 = jnp.float32
bf16 = jnp.bfloat16


def _cparams(sem):
    return pltpu.CompilerParams(dimension_semantics=sem, vmem_limit_bytes=VMEM_LIMIT)


def _largest_divisor(n, candidates):
    return next(c for c in candidates if n % c == 0)


def _rms_scale(x):
    return x * lax.rsqrt(jnp.mean(x * x, axis=-1, keepdims=True) + EPS)


def _norm_matmul_kernel(h_ref, g_ref, w_ref, o_ref, xn_ref, *, n_scaled, scale):
    j = pl.program_id(1)

    @pl.when(j == 0)
    def _():
        xn_ref[...] = (_rms_scale(h_ref[...]) * g_ref[...]).astype(bf16)

    mult = jnp.where(j < n_scaled, scale, 1.0)
    o_ref[...] = (jnp.dot(xn_ref[...], w_ref[...], preferred_element_type=f32) * mult).astype(o_ref.dtype)


def _norm_matmul(h, g, w, *, tm, tn, scaled_cols, scale):
    rows, d = h.shape
    n = w.shape[1]
    assert scaled_cols % tn == 0
    kern = functools.partial(_norm_matmul_kernel, n_scaled=scaled_cols // tn, scale=scale)
    return pl.pallas_call(
        kern,
        grid=(rows // tm, n // tn),
        in_specs=[
            pl.BlockSpec((tm, d), lambda i, j: (i, 0)),
            pl.BlockSpec((1, d), lambda i, j: (0, 0)),
            pl.BlockSpec((d, tn), lambda i, j: (0, j)),
        ],
        out_specs=pl.BlockSpec((tm, tn), lambda i, j: (i, j)),
        out_shape=jax.ShapeDtypeStruct((rows, n), bf16),
        scratch_shapes=[pltpu.VMEM((tm, d), bf16)],
        compiler_params=_cparams(("arbitrary", "arbitrary")),
        name="norm_qkv",
    )(h, g, w)


def _conv_in_kernel(x_ref, g_ref, wb_ref, wc_ref, wh_ref, cw_ref, lead_ref, o_ref, meta_v_ref,
                    xn_ref, carry_ref, *, tiles_per_seq):
    i = pl.program_id(0)
    j = pl.program_id(1)
    tm, tc = o_ref.shape

    @pl.when(j == 0)
    def _():
        xn_ref[...] = (_rms_scale(x_ref[...]) * g_ref[...]).astype(bf16)

    xn = xn_ref[...]
    b_gate = jnp.dot(xn, wb_ref[...], preferred_element_type=f32)
    v = (jnp.dot(xn, wc_ref[...], preferred_element_type=f32)
         * jnp.dot(xn, wh_ref[...], preferred_element_type=f32))

    seq_start = lax.rem(i, tiles_per_seq) == 0
    prev = jnp.where(seq_start, lead_ref[...], carry_ref[j])
    p1 = prev[SUBLANES - 1:SUBLANES, :]
    p2 = prev[SUBLANES - 2:SUBLANES - 1, :]
    row = lax.broadcasted_iota(jnp.int32, (tm, tc), 0)
    s1 = jnp.where(row == 0, p1, pltpu.roll(v, 1, 0))
    s2 = jnp.where(row == 0, p2, jnp.where(row == 1, p1, pltpu.roll(v, 2, 0)))
    y = cw_ref[0:1, :] * s2 + cw_ref[1:2, :] * s1 + cw_ref[2:3, :] * v
    o_ref[...] = (b_gate * y).astype(o_ref.dtype)

    carry_ref[j] = v[tm - SUBLANES:, :]

    meta_v_ref[...] = v[META_BLOCK - SUBLANES:META_BLOCK, :]


def _conv_in(x2d, g, w_in, conv_w, lead, *, seq_len, tm, tc):
    rows, d = x2d.shape
    n_j = d // tc
    assert seq_len % tm == 0 and rows % tm == 0
    kern = functools.partial(_conv_in_kernel, tiles_per_seq=seq_len // tm)
    return pl.pallas_call(
        kern,
        grid=(rows // tm, n_j),
        in_specs=[
            pl.BlockSpec((tm, d), lambda i, j: (i, 0)),
            pl.BlockSpec((1, d), lambda i, j: (0, 0)),
            pl.BlockSpec((d, tc), lambda i, j: (0, j)),
            pl.BlockSpec((d, tc), lambda i, j: (0, n_j + j)),
            pl.BlockSpec((d, tc), lambda i, j: (0, 2 * n_j + j)),
            pl.BlockSpec((CONV_WIDTH, tc), lambda i, j: (0, j)),
            pl.BlockSpec((SUBLANES, tc), lambda i, j: (0, j)),
        ],
        out_specs=[pl.BlockSpec((tm, tc), lambda i, j: (i, j)),
                   pl.BlockSpec((SUBLANES, tc), lambda i, j: (i, j))],
        out_shape=[jax.ShapeDtypeStruct((rows, d), bf16),
                   jax.ShapeDtypeStruct((SUBLANES * (rows // tm), d), f32)],
        scratch_shapes=[pltpu.VMEM((tm, d), bf16), pltpu.VMEM((n_j, SUBLANES, tc), f32)],
        compiler_params=_cparams(("arbitrary", "arbitrary")),
        name="conv_in",
    )(x2d, g, w_in, w_in, w_in, conv_w, lead)


def _matmul_residual_kernel(a_ref, w_ref, r_ref, o_ref):
    o_ref[...] = r_ref[...] + jnp.dot(a_ref[...], w_ref[...], preferred_element_type=f32)


def _matmul_residual_split_kernel(a_ref, at_ref, w_ref, r_ref, rt_ref, o_ref, *, n_main):
    @pl.when(pl.program_id(0) < n_main)
    def _():
        o_ref[...] = r_ref[...] + jnp.dot(a_ref[...], w_ref[...], preferred_element_type=f32)

    @pl.when(pl.program_id(0) >= n_main)
    def _():
        o_ref[...] = rt_ref[...] + jnp.dot(at_ref[...], w_ref[...], preferred_element_type=f32)


def _matmul_residual(a, w, r, *, tm):
    rows, k = a.shape
    n = w.shape[1]
    return pl.pallas_call(
        _matmul_residual_kernel,
        grid=(rows // tm,),
        in_specs=[pl.BlockSpec((tm, k), lambda i: (i, 0)),
                  pl.BlockSpec((k, n), lambda i: (0, 0)),
                  pl.BlockSpec((tm, n), lambda i: (i, 0))],
        out_specs=pl.BlockSpec((tm, n), lambda i: (i, 0)),
        out_shape=jax.ShapeDtypeStruct((rows, n), f32),
        compiler_params=_cparams(("arbitrary",)),
        name="out_proj_residual",
    )(a, w, r)


def _matmul_residual_split(a, a_tail, w, r, r_tail, *, tm):
    rows, k = a.shape
    n = w.shape[1]
    n_main = rows // tm
    assert rows % tm == 0 and a_tail.shape[0] == tm and r.shape[0] == rows and r_tail.shape[0] == tm
    kern = functools.partial(_matmul_residual_split_kernel, n_main=n_main)

    def main_blk(i):
        return (jnp.minimum(i, n_main - 1), 0)

    return pl.pallas_call(
        kern,
        grid=(n_main + 1,),
        in_specs=[pl.BlockSpec((tm, k), main_blk),
                  pl.BlockSpec((tm, k), lambda i: (0, 0)),
                  pl.BlockSpec((k, n), lambda i: (0, 0)),
                  pl.BlockSpec((tm, n), main_blk),
                  pl.BlockSpec((tm, n), lambda i: (0, 0))],
        out_specs=pl.BlockSpec((tm, n), lambda i: (i, 0)),
        out_shape=jax.ShapeDtypeStruct((rows + tm, n), f32),
        compiler_params=_cparams(("arbitrary",)),
        name="out_proj_residual",
    )(a, a_tail, w, r, r_tail)


def _sb_scores(q, kb):
    return lax.dot_general(kb, q, (((1,), (1,)), ((), ())), preferred_element_type=f32)


def _sb_logs(z, mask):
    softplus_neg = jnp.log(1.0 + jnp.exp2(jnp.abs(z) * (-LOG2E)))
    log_sig = jnp.minimum(z, 0.0) - softplus_neg
    log_1m = log_sig - z
    if mask is not None:
        log_1m = jnp.where(mask, log_1m, 0.0)
    return log_sig, jnp.sum(log_1m, axis=0, keepdims=True), log_1m.astype(bf16)


def _sb_suffix_sums(u, log_1m):
    return jnp.dot(u, log_1m, preferred_element_type=f32)


def _sb_weights(log_sig, log_1m_sum, rest, carry, mask):
    a = jnp.exp2((log_sig + rest + carry) * LOG2E)
    if mask is not None:
        a = jnp.where(mask, a, 0.0)
    return a.astype(bf16), carry + log_1m_sum


def _sb_values(vb, a):
    return lax.dot_general(vb, a, (((0,), (0,)), ((), ())), preferred_element_type=f32)


def _sb_block(q, kb, vb, u, carry, mask):
    log_sig, total, log_1m = _sb_logs(_sb_scores(q, kb), mask)
    a, carry = _sb_weights(log_sig, total, _sb_suffix_sums(u, log_1m), carry, mask)
    return _sb_values(vb, a), carry


def _attn_kernel(q_ref, k_ref, v_ref, km_ref, vm_ref, u_ref, o_ref, acc_ref, carry_ref, *,
                 n_pad, n_main_steps):
    step = pl.program_id(1)
    steps_per_seq = k_ref.shape[0] // (ATT_QSUB * ATT_BLOCK)
    qstep = lax.rem(step, steps_per_seq)
    tb = ATT_BLOCK
    dh = q_ref.shape[1] // ATT_HEADS
    u = u_ref[...]
    s_loc = lax.broadcasted_iota(jnp.int32, (tb, tb), 0)
    t_loc = lax.broadcasted_iota(jnp.int32, (tb, tb), 1)
    causal = s_loc < t_loc
    meta_keys = (s_loc >= n_pad) & (s_loc < META_BLOCK)

    @pl.when(step < n_main_steps)
    def _():
        chains = [(hh, qs) for hh in range(ATT_HEADS) for qs in range(ATT_QSUB)]
        first = qstep == 0

        def keys(ref, mref, hh, qs, prev):
            cols = slice(hh * dh, (hh + 1) * dh)
            i = qstep * ATT_QSUB + qs
            blk = jnp.maximum(i - 1, 0) if prev else i
            kv = ref[pl.ds(pl.multiple_of(blk * tb, tb), tb), cols]
            if prev and qs == 0:
                kv = jnp.where(first, mref[:, cols], kv)
            return kv

        items = [(hh, qs, False) for hh, qs in chains] + [(hh, qs, True) for hh, qs in chains]
        n_items = len(items)
        masks = [(meta_keys | jnp.logical_not(first)) if (prev and qs == 0)
                 else (None if prev else causal) for hh, qs, prev in items]
        z, logs, rest, wts, contrib, carry_out = {}, {}, {}, {}, {}, {}
        for t in range(n_items + 4):
            if 0 <= t - 4 < n_items:
                n = t - 4
                hh, qs, prev = items[n]
                contrib[n] = _sb_values(keys(v_ref, vm_ref, hh, qs, prev), wts.pop(n))
            if 0 <= t - 3 < n_items:
                n = t - 3
                hh, qs, prev = items[n]
                carry_in = carry_out[n - len(chains)] if prev else jnp.zeros((1, tb), f32)
                log_sig, row0, _ = logs.pop(n)
                wts[n], carry_out[n] = _sb_weights(log_sig, row0, rest.pop(n), carry_in, masks[n])
            if 0 <= t - 2 < n_items:
                rest[t - 2] = _sb_suffix_sums(u, logs[t - 2][2])
            if 0 <= t - 1 < n_items:
                logs[t - 1] = _sb_logs(z.pop(t - 1), masks[t - 1])
            if t < n_items:
                hh, qs, prev = items[t]
                q = q_ref[qs * tb:(qs + 1) * tb, hh * dh:(hh + 1) * dh]
                z[t] = _sb_scores(q, keys(k_ref, km_ref, hh, qs, prev))
        for c, (hh, qs) in enumerate(chains):
            acc_ref[hh, qs] = contrib[c] + contrib[c + len(chains)]
            carry_ref[hh, qs] = carry_out[c + len(chains)]

        top_all = jnp.max(functools.reduce(
            jnp.maximum, [carry_out[c + len(chains)] for c in range(len(chains))]))

        @pl.when(top_all > SKIP_BELOW)
        def _():
            for hh, qs in chains:
                cols = slice(hh * dh, (hh + 1) * dh)
                i = qstep * ATT_QSUB + qs

                def cond(state):
                    j, top = state
                    return (j >= 0) & (top > SKIP_BELOW)

                def body(state, hh=hh, qs=qs, cols=cols):
                    j, _ = state
                    st = pl.multiple_of(j * tb, tb)
                    q = q_ref[qs * tb:(qs + 1) * tb, cols]
                    contrib, carry = _sb_block(q, k_ref[pl.ds(st, tb), cols],
                                               v_ref[pl.ds(st, tb), cols], u, carry_ref[hh, qs],
                                               None)
                    acc_ref[hh, qs] += contrib
                    carry_ref[hh, qs] = carry
                    return j - 1, jnp.max(carry)

                _, top = lax.while_loop(cond, body, (i - 2, jnp.max(carry_ref[hh, qs])))

                @pl.when((top > SKIP_BELOW) & (i >= 1))
                def _(hh=hh, qs=qs, cols=cols):
                    q = q_ref[qs * tb:(qs + 1) * tb, cols]
                    contrib, _ = _sb_block(q, km_ref[:, cols], vm_ref[:, cols], u,
                                           carry_ref[hh, qs], meta_keys)
                    acc_ref[hh, qs] += contrib

        for hh, qs in chains:
            o_ref[qs * tb:(qs + 1) * tb, hh * dh:(hh + 1) * dh] = acc_ref[hh, qs].T.astype(o_ref.dtype)

    @pl.when(step >= n_main_steps)
    def _():
        o_ref[...] = jnp.zeros(o_ref.shape, o_ref.dtype)
        for hh in range(ATT_HEADS):
            cols = slice(hh * dh, (hh + 1) * dh)
            contrib, _ = _sb_block(q_ref[0:tb, cols], km_ref[:, cols], vm_ref[:, cols], u,
                                   jnp.zeros((1, tb), f32), causal & meta_keys)
            o_ref[0:tb, cols] = contrib.T.astype(o_ref.dtype)


def _attention(qkv, u, *, batch, seq_len, d_model, n_pad):
    rows = qkv.shape[0]
    dh = d_model // N_HEADS
    tqs = ATT_QSUB * ATT_BLOCK
    assert tqs == TAIL_ROWS and seq_len % tqs == 0 and N_HEADS % ATT_HEADS == 0
    steps_per_seq = seq_len // tqs
    n_main = batch * steps_per_seq
    n_hp = N_HEADS // ATT_HEADS
    wcols = ATT_HEADS * dh
    meta_blk = (batch * seq_len) // ATT_BLOCK
    kern = functools.partial(_attn_kernel, n_pad=n_pad, n_main_steps=n_main)

    def seq_of(s):
        return jnp.minimum(s // steps_per_seq, batch - 1)

    return pl.pallas_call(
        kern,
        grid=(n_hp, n_main + 1),
        in_specs=[
            pl.BlockSpec((tqs, wcols), lambda h, s: (s, h)),
            pl.BlockSpec((seq_len, wcols), lambda h, s: (seq_of(s), n_hp + h)),
            pl.BlockSpec((seq_len, wcols), lambda h, s: (seq_of(s), 2 * n_hp + h)),
            pl.BlockSpec((ATT_BLOCK, wcols), lambda h, s: (meta_blk, n_hp + h)),
            pl.BlockSpec((ATT_BLOCK, wcols), lambda h, s: (meta_blk, 2 * n_hp + h)),
            pl.BlockSpec((ATT_BLOCK, ATT_BLOCK), lambda h, s: (0, 0)),
        ],
        out_specs=pl.BlockSpec((tqs, wcols), lambda h, s: (s, h)),
        out_shape=jax.ShapeDtypeStruct((rows, d_model), bf16),
        scratch_shapes=[pltpu.VMEM((ATT_HEADS, ATT_QSUB, dh, ATT_BLOCK), f32),
                        pltpu.VMEM((ATT_HEADS, ATT_QSUB, 1, ATT_BLOCK), f32)],
        compiler_params=_cparams(("arbitrary", "arbitrary")),
        name="sb_attention",
    )(qkv, qkv, qkv, qkv, qkv, u)


def _pack_rows(x, w_ref, z_ref):
    m, d = x.shape
    half = d // 2
    for c in range(half // LANES):
        z_ref[pl.ds(0, m, stride=2), :] = x[:, c * LANES:(c + 1) * LANES]
        z_ref[pl.ds(1, m, stride=2), :] = x[:, half + c * LANES:half + (c + 1) * LANES]
        words = pltpu.bitcast(z_ref[...].astype(bf16), jnp.uint32)
        w_ref[:, c, :, :] = words.reshape(m // SUBLANES, SUBLANES, LANES)


def _unpack_rows(w_ref, z_ref):
    m = w_ref.shape[0] * SUBLANES
    lo, hi = [], []
    for c in range(w_ref.shape[1]):
        words = w_ref[:, c, :, :].reshape(m, LANES)
        z_ref[...] = pltpu.bitcast(words, bf16).astype(f32)
        lo.append(z_ref[pl.ds(0, m, stride=2), :].astype(bf16))
        hi.append(z_ref[pl.ds(1, m, stride=2), :].astype(bf16))
    return jnp.concatenate(lo + hi, axis=1)


def _row_tiled(rows, width):
    return (rows // SUBLANES, width // LANES, SUBLANES, LANES)


def _row_of(ref, r):
    return ref.at[lax.shift_right_logical(r, LOG2_SUBLANES), :, pl.ds(r & (SUBLANES - 1), 1), :]


def _router_kernel(h_ref, g_ref, w_ref, b_ref, xp_ref, info_ref, infot_ref, cnt_ref, base_ref,
                   z_ref, *, n_experts):
    i = pl.program_id(0)
    tm = h_ref.shape[0]
    epg = n_experts // N_GROUPS

    @pl.when(i == 0)
    def _():
        base_ref[...] = jnp.zeros(base_ref.shape, f32)

    xn = _rms_scale(h_ref[...]) * g_ref[...]
    _pack_rows(xn, xp_ref, z_ref)

    xh = xn.astype(bf16)
    xl = (xn - xh.astype(f32)).astype(bf16)
    hh_hl = jnp.dot(xh, w_ref[...], preferred_element_type=f32)
    logits = (hh_hl[:, :LANES] + hh_hl[:, LANES:]
              + jnp.dot(xl, w_ref[:, :LANES], preferred_element_type=f32) + b_ref[...])
    lane = lax.broadcasted_iota(jnp.int32, logits.shape, 1)
    neg = jnp.float32(-jnp.inf)
    big = jnp.int32(LANES)

    gl = jnp.where(lane < N_GROUPS, logits, neg)
    gmax = jnp.max(gl, axis=-1, keepdims=True)
    g_sel = jnp.min(jnp.where(gl == gmax, lane, big), axis=-1, keepdims=True)
    p_top = 1.0 / jnp.sum(jnp.exp(gl - gmax), axis=-1, keepdims=True)

    lo_lane = N_GROUPS + g_sel * epg
    el = jnp.where((lane >= lo_lane) & (lane < lo_lane + epg), logits, neg)
    v1 = jnp.max(el, axis=-1, keepdims=True)
    l1 = jnp.min(jnp.where(el == v1, lane, big), axis=-1, keepdims=True)
    el2 = jnp.where(lane == l1, neg, el)
    v2 = jnp.max(el2, axis=-1, keepdims=True)
    l2 = jnp.min(jnp.where(el2 == v2, lane, big), axis=-1, keepdims=True)
    r = jnp.exp(v2 - v1)
    gate1 = p_top * (1.0 / (1.0 + r))
    gate2 = p_top * (r / (1.0 + r))

    onehot = ((lane == l1) | (lane == l2)).astype(f32)
    r_i = lax.broadcasted_iota(jnp.int32, (tm, tm), 0)
    c_i = lax.broadcasted_iota(jnp.int32, (tm, tm), 1)
    earlier = jnp.where(c_i < r_i, 1.0, 0.0).astype(bf16)
    before = jnp.dot(earlier, onehot.astype(bf16), preferred_element_type=f32) + base_ref[...]
    rank1 = jnp.sum(jnp.where(lane == l1, before, 0.0), axis=-1, keepdims=True)
    rank2 = jnp.sum(jnp.where(lane == l2, before, 0.0), axis=-1, keepdims=True)
    base_ref[...] = base_ref[...] + jnp.sum(onehot, axis=0, keepdims=True)
    cnt_ref[...] = base_ref[...].astype(jnp.int32)

    e1 = jnp.clip(l1 - N_GROUPS, 0, n_experts - 1)
    e2 = jnp.clip(l2 - N_GROUPS, 0, n_experts - 1)
    info = jnp.where(lane == 0, e1, 0)
    info = jnp.where(lane == 1, e2, info)
    info = jnp.where(lane == 2, rank1.astype(jnp.int32), info)
    info = jnp.where(lane == 3, rank2.astype(jnp.int32), info)
    info = jnp.where(lane == 4, pltpu.bitcast(jnp.broadcast_to(gate1, logits.shape), jnp.int32), info)
    info = jnp.where(lane == 5, pltpu.bitcast(jnp.broadcast_to(gate2, logits.shape), jnp.int32), info)
    info_ref[...] = info
    infot_ref[...] = info.T[0:INFO_ROWS, :]


def _router(h, g, w_r, b_r, *, n_experts, tm):
    rows, d = h.shape
    kern = functools.partial(_router_kernel, n_experts=n_experts)
    return pl.pallas_call(
        kern,
        grid=(rows // tm,),
        in_specs=[
            pl.BlockSpec((tm, d), lambda i: (i, 0)),
            pl.BlockSpec((1, d), lambda i: (0, 0)),
            pl.BlockSpec((d, 2 * LANES), lambda i: (0, 0)),
            pl.BlockSpec((1, LANES), lambda i: (0, 0)),
        ],
        out_specs=[
            pl.BlockSpec(_row_tiled(tm, d // 2), lambda i: (i, 0, 0, 0)),
            pl.BlockSpec((tm, LANES), lambda i: (i, 0)),
            pl.BlockSpec((INFO_ROWS, tm), lambda i: (0, i)),
            pl.BlockSpec((1, LANES), lambda i: (0, 0)),
        ],
        out_shape=[
            jax.ShapeDtypeStruct(_row_tiled(rows, d // 2), jnp.uint32),
            jax.ShapeDtypeStruct((rows, LANES), jnp.int32),
            jax.ShapeDtypeStruct((INFO_ROWS, rows), jnp.int32),
            jax.ShapeDtypeStruct((1, LANES), jnp.int32),
        ],
        scratch_shapes=[pltpu.VMEM((1, LANES), f32), pltpu.VMEM((2 * tm, LANES), f32)],
        compiler_params=_cparams(("arbitrary",)),
        name="moe_router",
    )(h, g, w_r, b_r)


def _pad_fill_copies(fs_ref, pad_ref, zbuf, xs_ref, sem, e):
    fs = fs_ref[e]
    pad = pad_ref[e]
    head = pad & (SUBLANES - 1)
    out = []
    for k in range(SUBLANES - 1):
        out.append((k < head,
                    pltpu.make_async_copy(zbuf.at[0, :, pl.ds(0, 1), :], _row_of(xs_ref, fs + k), sem)))
    cur = lax.shift_right_logical(fs + head, LOG2_SUBLANES)
    for b in reversed(range(SUBLANES.bit_length() - 1, MOE_TILE.bit_length() - 1)):
        size = 1 << b
        groups = size // SUBLANES
        out.append(((pad & size) != 0,
                    pltpu.make_async_copy(zbuf.at[pl.ds(0, groups)], xs_ref.at[pl.ds(cur, groups)], sem)))
        cur = cur + lax.shift_right_logical(pad & size, LOG2_SUBLANES)
    return out


def _unused_tile_copies(zbuf, xs_ref, sem, t):
    groups = zbuf.shape[0]
    per_tile = MOE_TILE // SUBLANES
    return [pltpu.make_async_copy(zbuf, xs_ref.at[pl.ds(t * per_tile + k * groups, groups)], sem)
            for k in range(per_tile // groups)]


def _dispatch_kernel(p0_ref, p1_ref, fs_ref, pad_ref, nu_ref, x_ref, xs_ref, zbuf, sem, zsem, *,
                     n_steps, n_experts, max_tiles):
    i = pl.program_id(0)
    tm = x_ref.shape[0] * SUBLANES
    base = i * tm

    @pl.when(i == 0)
    def _():
        zbuf[...] = jnp.zeros(zbuf.shape, zbuf.dtype)

        def fill(e, c):
            for needed, copy in _pad_fill_copies(fs_ref, pad_ref, zbuf, xs_ref, zsem, e):
                @pl.when(needed)
                def _():
                    copy.start()
            return c

        lax.fori_loop(0, n_experts, fill, 0)

        def fill_unused(t, c):
            for copy in _unused_tile_copies(zbuf, xs_ref, zsem, t):
                copy.start()
            return c

        lax.fori_loop(nu_ref[0], max_tiles, fill_unused, 0)

    def issue(g, c):
        for k in range(SUBLANES):
            src = x_ref.at[g, :, pl.ds(k, 1), :]
            r = base + g * SUBLANES + k
            pltpu.make_async_copy(src, _row_of(xs_ref, p0_ref[r]), sem).start(priority=0)
            pltpu.make_async_copy(src, _row_of(xs_ref, p1_ref[r]), sem).start(priority=1)
        return c

    lax.fori_loop(0, tm // SUBLANES, issue, 0)
    for _ in range(TOP_K):
        pltpu.make_async_copy(x_ref, xs_ref.at[pl.ds(0, tm // SUBLANES)], sem).wait()

    @pl.when(i == n_steps - 1)
    def _():
        def drain(e, c):
            for needed, copy in _pad_fill_copies(fs_ref, pad_ref, zbuf, xs_ref, zsem, e):
                @pl.when(needed)
                def _():
                    copy.wait()
            return c

        lax.fori_loop(0, n_experts, drain, 0)

        def drain_unused(t, c):
            for copy in _unused_tile_copies(zbuf, xs_ref, zsem, t):
                copy.wait()
            return c

        lax.fori_loop(nu_ref[0], max_tiles, drain_unused, 0)


def _dispatch(pos0, pos1, fill_start, pad, n_used, xp, *, sorted_rows, tm):
    width = xp.shape[1] * LANES
    n_steps = (xp.shape[0] * SUBLANES) // tm
    kern = functools.partial(_dispatch_kernel, n_steps=n_steps, n_experts=pad.shape[0],
                             max_tiles=sorted_rows // MOE_TILE)
    return pl.pallas_call(
        kern,
        grid_spec=pltpu.PrefetchScalarGridSpec(
            num_scalar_prefetch=5,
            grid=(n_steps,),
            in_specs=[pl.BlockSpec(_row_tiled(tm, width), lambda i, *_: (i, 0, 0, 0))],
            out_specs=pl.BlockSpec(memory_space=pl.ANY),
            scratch_shapes=[pltpu.VMEM(_row_tiled(MOE_TILE // 2, width), jnp.uint32),
                            pltpu.SemaphoreType.DMA(()), pltpu.SemaphoreType.DMA(())],
        ),
        out_shape=jax.ShapeDtypeStruct(_row_tiled(sorted_rows, width), jnp.uint32),
        compiler_params=pltpu.CompilerParams(dimension_semantics=("arbitrary",),
                                             vmem_limit_bytes=VMEM_LIMIT, has_side_effects=True),
        name="moe_dispatch",
    )(pos0, pos1, fill_start, pad, n_used, xp)


def _experts_kernel(te_ref, tf_ref, nx_ref, nu_ref, xs_ref, wg_hbm, wu_hbm, wd_hbm, y_ref,
                    wg_f, wu_f, wd_f, wg_b, wu_b, wd_b, z_ref, slot_ref, sems, *, layer):
    t = pl.program_id(0)

    def weight_copies(e, slot):
        return [pltpu.make_async_copy(hbm.at[layer, e], buf.at[slot], sems.at[slot, k])
                for k, (hbm, buf) in enumerate(((wg_hbm, wg_f), (wu_hbm, wu_f), (wd_hbm, wd_f)))]

    @pl.when(t == 0)
    def _():
        slot_ref[0] = 0
        for c in weight_copies(te_ref[0], 0):
            c.start()

    @pl.when(tf_ref[t] == 1)
    def _():
        slot = slot_ref[0]
        for c in weight_copies(te_ref[t], slot):
            c.wait()

        @pl.when(nx_ref[t] >= 0)
        def _():
            for c in weight_copies(nx_ref[t], 1 - slot):
                c.start()

        wg_b[...] = wg_f[slot].astype(bf16)
        wu_b[...] = wu_f[slot].astype(bf16)
        wd_b[...] = wd_f[slot].astype(bf16)
        slot_ref[0] = 1 - slot

    @pl.when(t < nu_ref[0])
    def _():
        x = _unpack_rows(xs_ref, z_ref)
        gate = jnp.dot(x, wg_b[...], preferred_element_type=f32)
        up = jnp.dot(x, wu_b[...], preferred_element_type=f32)
        hidden = (gate * (1.0 / (1.0 + jnp.exp(-gate))) * up).astype(bf16)
        _pack_rows(jnp.dot(hidden, wd_b[...], preferred_element_type=f32), y_ref, z_ref)

    @pl.when(t >= nu_ref[0])
    def _():
        y_ref[...] = jnp.zeros(y_ref.shape, y_ref.dtype)


def _experts(tile_expert, tile_first, next_expert, n_used, xs, w_gate, w_up, w_down, *, layer,
             max_tiles):
    tm = MOE_TILE
    _, _, d, f = w_gate.shape
    kern = functools.partial(_experts_kernel, layer=layer)
    hbm = pl.BlockSpec(memory_space=pl.ANY)
    return pl.pallas_call(
        kern,
        grid_spec=pltpu.PrefetchScalarGridSpec(
            num_scalar_prefetch=4,
            grid=(max_tiles,),
            in_specs=[pl.BlockSpec(_row_tiled(tm, d // 2),
                                   lambda t, te, tf, nx, nu: (jnp.clip(nu[0] - 1, 0, t), 0, 0, 0)),
                      hbm, hbm, hbm],
            out_specs=pl.BlockSpec(_row_tiled(tm, d // 2), lambda t, *_: (t, 0, 0, 0)),
            scratch_shapes=[pltpu.VMEM((2, d, f), f32), pltpu.VMEM((2, d, f), f32),
                            pltpu.VMEM((2, f, d), f32),
                            pltpu.VMEM((d, f), bf16), pltpu.VMEM((d, f), bf16),
                            pltpu.VMEM((f, d), bf16), pltpu.VMEM((2 * tm, LANES), f32),
                            pltpu.SMEM((1,), jnp.int32), pltpu.SemaphoreType.DMA((2, 3))],
        ),
        out_shape=jax.ShapeDtypeStruct(_row_tiled(max_tiles * tm, d // 2), jnp.uint32),
        compiler_params=_cparams(("arbitrary",)),
        name="moe_experts",
    )(tile_expert, tile_first, next_expert, n_used, xs, w_gate, w_up, w_down)


def _combine_kernel(p0_ref, p1_ref, h_ref, info_ref, gf_ref, ys_ref, o_ref, buf, z_ref, sems, *,
                    n_steps, final_norm):
    i = pl.program_id(0)
    tm = h_ref.shape[0]

    def issue(step, slot):
        base = step * tm

        def body(g, c):
            for k in range(SUBLANES):
                r = base + g * SUBLANES + k
                pltpu.make_async_copy(_row_of(ys_ref, p0_ref[r]),
                                      buf.at[slot, 0, g, :, pl.ds(k, 1), :],
                                      sems.at[slot]).start(priority=0)
                pltpu.make_async_copy(_row_of(ys_ref, p1_ref[r]),
                                      buf.at[slot, 1, g, :, pl.ds(k, 1), :],
                                      sems.at[slot]).start(priority=1)
            return c

        lax.fori_loop(0, tm // SUBLANES, body, 0)

    @pl.when(i == 0)
    def _():
        issue(0, 0)

    @pl.when(i + 1 < n_steps)
    def _():
        issue(i + 1, lax.rem(i + 1, 2))

    slot = lax.rem(i, 2)
    for k in range(TOP_K):
        pltpu.make_async_copy(ys_ref.at[pl.ds(0, tm // SUBLANES)], buf.at[slot, k],
                              sems.at[slot]).wait()

    info = info_ref[...]
    lane = lax.broadcasted_iota(jnp.int32, info.shape, 1)
    gates = pltpu.bitcast(info, f32)
    g1 = jnp.sum(jnp.where(lane == 4, gates, 0.0), axis=-1, keepdims=True)
    g2 = jnp.sum(jnp.where(lane == 5, gates, 0.0), axis=-1, keepdims=True)
    lo, hi = [], []
    for c in range(buf.shape[3]):
        for k in range(TOP_K):
            words = buf[slot, k, :, c, :, :].reshape(tm, LANES)
            z_ref[k] = pltpu.bitcast(words, bf16).astype(f32)
        lo.append(g1 * z_ref[0, pl.ds(0, tm, stride=2), :] + g2 * z_ref[1, pl.ds(0, tm, stride=2), :])
        hi.append(g1 * z_ref[0, pl.ds(1, tm, stride=2), :] + g2 * z_ref[1, pl.ds(1, tm, stride=2), :])
    out = h_ref[...] + jnp.concatenate(lo + hi, axis=1)
    if final_norm:
        out = _rms_scale(out) * gf_ref[...]
    o_ref[...] = out


def _combine(pos0, pos1, h, info, gf, ys, *, out_rows, tm, final_norm):
    d = h.shape[1]
    n_steps = out_rows // tm
    kern = functools.partial(_combine_kernel, n_steps=n_steps, final_norm=final_norm)
    return pl.pallas_call(
        kern,
        grid_spec=pltpu.PrefetchScalarGridSpec(
            num_scalar_prefetch=2,
            grid=(n_steps,),
            in_specs=[
                pl.BlockSpec((tm, d), lambda i, p0, p1: (i, 0)),
                pl.BlockSpec((tm, LANES), lambda i, p0, p1: (i, 0)),
                pl.BlockSpec((1, d), lambda i, p0, p1: (0, 0)),
                pl.BlockSpec(memory_space=pl.ANY),
            ],
            out_specs=pl.BlockSpec((tm, d), lambda i, p0, p1: (i, 0)),
            scratch_shapes=[pltpu.VMEM((2, TOP_K) + _row_tiled(tm, d // 2), jnp.uint32),
                            pltpu.VMEM((TOP_K, 2 * tm, LANES), f32),
                            pltpu.SemaphoreType.DMA((2,))],
        ),
        out_shape=jax.ShapeDtypeStruct((out_rows, d), f32),
        compiler_params=_cparams(("arbitrary",)),
        name="moe_combine",
    )(pos0, pos1, h, info, gf, ys)


def _table_lookup(table, idx):
    k = lax.broadcasted_iota(jnp.int32, (table.shape[0], idx.shape[0]), 0)
    return jnp.sum(jnp.where(k == idx[None, :], table[:, None], 0), axis=0)


def _moe_layer(h, g_ffn, wg_r, bg_r, we_r, be_r, w_gate, w_up, w_down, gf, *, layer, out_rows,
               final_norm):
    rows, d = h.shape
    n_experts = we_r.shape[1]
    n_fill = LANES - N_GROUPS - n_experts
    w_r = jnp.concatenate([wg_r, we_r, jnp.zeros((d, n_fill), f32)], axis=1)
    b_r = jnp.concatenate([bg_r, be_r, jnp.zeros((n_fill,), f32)]).reshape(1, LANES)
    w_hi = w_r.astype(bf16)
    w_hl = jnp.concatenate([w_hi, (w_r - w_hi.astype(f32)).astype(bf16)], axis=1)

    xp, info, info_t, cnt = _router(h, g_ffn.reshape(1, d), w_hl, b_r, n_experts=n_experts, tm=512)

    tm = MOE_TILE
    max_tiles = (TOP_K * rows) // tm + n_experts
    i32 = jnp.int32
    cnt = cnt[0, N_GROUPS:N_GROUPS + n_experts]
    n_tiles = (cnt + tm - 1) // tm
    tile_end = jnp.cumsum(n_tiles)
    tile_start = tile_end - n_tiles
    n_used = tile_end[-1]
    t_ids = jnp.arange(max_tiles, dtype=i32)
    used = t_ids < n_used
    te = jnp.minimum(jnp.sum(tile_end[None, :] <= t_ids[:, None], axis=1), n_experts - 1)
    tile_first = (used & (t_ids == tile_start[te])).astype(i32)
    nxt_tile = tile_end[te]
    next_expert = jnp.where(nxt_tile < n_used, te[jnp.minimum(nxt_tile, max_tiles - 1)], -1)

    sorted_rows = max_tiles * tm
    row_off = (tile_start * tm).astype(i32)
    pos0 = jnp.clip(_table_lookup(row_off, info_t[0]) + info_t[2], 0, sorted_rows - 1)
    pos1 = jnp.clip(_table_lookup(row_off, info_t[1]) + info_t[3], 0, sorted_rows - 1)

    n_used = n_used.reshape(1).astype(i32)
    xs = _dispatch(pos0, pos1, (row_off + cnt).astype(i32), (n_tiles * tm - cnt).astype(i32), n_used,
                   xp, sorted_rows=sorted_rows, tm=512)
    ys = _experts(te.astype(i32), tile_first, next_expert.astype(i32), n_used,
                  xs, w_gate, w_up, w_down, layer=layer, max_tiles=max_tiles)
    return _combine(pos0, pos1, h, info, gf.reshape(1, d), ys, out_rows=out_rows, tm=256,
                    final_norm=final_norm)


def _strict_upper_ones(n):
    r = lax.broadcasted_iota(jnp.int32, (n, n), 0)
    c = lax.broadcasted_iota(jnp.int32, (n, n), 1)
    return (c > r).astype(bf16)


def kernel(x, meta_tokens, norm_mix_g, norm_ffn_g, conv_w_in, conv_w, conv_w_out, attn_w_qkv, attn_w_out, router_group_w, router_group_b, router_expert_w, router_expert_b, moe_w_gate, moe_w_up, moe_w_down, final_norm_g):
    batch, seq_len, d = x.shape
    n_meta = meta_tokens.shape[0]
    n_pad = META_BLOCK - n_meta
    n_tok = batch * seq_len
    assert seq_len % TAIL_ROWS == 0 and d % (2 * N_HEADS) == 0

    x2d = x.reshape(n_tok, d)
    tail = jnp.concatenate([jnp.zeros((n_pad, d), f32), meta_tokens.astype(f32),
                            jnp.zeros((TAIL_ROWS - META_BLOCK, d), f32)], axis=0)

    tc = _largest_divisor(d, (512, 256, 128))
    g_mix0 = norm_mix_g[0].reshape(1, d)
    w_in = conv_w_in[0].astype(bf16)
    gated_tail, meta_v = _conv_in(tail, g_mix0, w_in, conv_w[0], jnp.zeros((SUBLANES, d), f32),
                                  seq_len=TAIL_ROWS, tm=TAIL_ROWS, tc=tc)
    gated, _ = _conv_in(x2d, g_mix0, w_in, conv_w[0], meta_v, seq_len=seq_len,
                        tm=_largest_divisor(seq_len, (1024, 512)), tc=tc)
    h = _matmul_residual_split(gated, gated_tail, conv_w_out[0].astype(bf16), x2d, tail,
                               tm=TAIL_ROWS)
    h = _moe_layer(h, norm_ffn_g[0], router_group_w[0], router_group_b[0], router_expert_w[0],
                   router_expert_b[0], moe_w_gate, moe_w_up, moe_w_down, final_norm_g,
                   layer=0, out_rows=h.shape[0], final_norm=False)

    qkv = _norm_matmul(h, norm_mix_g[1].reshape(1, d), attn_w_qkv[0].astype(bf16),
                       tm=_largest_divisor(h.shape[0], (1536, 512)),
                       tn=_largest_divisor(d, (1024, 512, 256, 128)),
                       scaled_cols=d, scale=(d // N_HEADS) ** -0.5)
    o = _attention(qkv, _strict_upper_ones(ATT_BLOCK), batch=batch, seq_len=seq_len, d_model=d,
                   n_pad=n_pad)
    h = _matmul_residual(o, attn_w_out[0].astype(bf16), h, tm=TAIL_ROWS)
    out = _moe_layer(h, norm_ffn_g[1], router_group_w[1], router_group_b[1], router_expert_w[1],
                     router_expert_b[1], moe_w_gate, moe_w_up, moe_w_down, final_norm_g,
                     layer=1, out_rows=n_tok, final_norm=True)
    return out.reshape(batch, seq_len, d)
```

```python
import functools

import jax
import jax.numpy as jnp
from jax import lax
from jax.experimental import pallas as pl
from jax.experimental.pallas import tpu as pltpu

N_HEADS = 16
N_GROUPS = 4
TOP_K = 2
CONV_WIDTH = 3
EPS = 1e-6

LANES = 128
SUBLANES = 8
LOG2_SUBLANES = SUBLANES.bit_length() - 1
META_BLOCK = 128
TAIL_ROWS = 512
MOE_TILE = 256
ATT_BLOCK = 256
ATT_HEADS = 8
ATT_QSUB = 2
INFO_ROWS = 8
SKIP_BELOW = -104.0
LOG2E = 1.4426950408889634
VMEM_LIMIT = 56 * 1024 * 1024

f32 = jnp.float32
bf16 = jnp.bfloat16


def _cparams(sem):
    return pltpu.CompilerParams(dimension_semantics=sem, vmem_limit_bytes=VMEM_LIMIT)


def _largest_divisor(n, candidates):
    return next(c for c in candidates if n % c == 0)


def _rms_scale(x):
    return x * lax.rsqrt(jnp.mean(x * x, axis=-1, keepdims=True) + EPS)


def _norm_matmul_kernel(h_ref, g_ref, w_ref, o_ref, xn_ref, *, n_scaled, scale):
    j = pl.program_id(1)

    @pl.when(j == 0)
    def _():
        xn_ref[...] = (_rms_scale(h_ref[...]) * g_ref[...]).astype(bf16)

    mult = jnp.where(j < n_scaled, scale, 1.0)
    o_ref[...] = (jnp.dot(xn_ref[...], w_ref[...], preferred_element_type=f32) * mult).astype(o_ref.dtype)


def _norm_matmul(h, g, w, *, tm, tn, scaled_cols, scale):
    rows, d = h.shape
    n = w.shape[1]
    assert scaled_cols % tn == 0
    kern = functools.partial(_norm_matmul_kernel, n_scaled=scaled_cols // tn, scale=scale)
    return pl.pallas_call(
        kern,
        grid=(rows // tm, n // tn),
        in_specs=[
            pl.BlockSpec((tm, d), lambda i, j: (i, 0)),
            pl.BlockSpec((1, d), lambda i, j: (0, 0)),
            pl.BlockSpec((d, tn), lambda i, j: (0, j)),
        ],
        out_specs=pl.BlockSpec((tm, tn), lambda i, j: (i, j)),
        out_shape=jax.ShapeDtypeStruct((rows, n), bf16),
        scratch_shapes=[pltpu.VMEM((tm, d), bf16)],
        compiler_params=_cparams(("arbitrary", "arbitrary")),
        name="norm_qkv",
    )(h, g, w)


def _conv_in_kernel(x_ref, g_ref, wb_ref, wc_ref, wh_ref, cw_ref, lead_ref, o_ref, meta_v_ref,
                    xn_ref, carry_ref, *, tiles_per_seq):
    i = pl.program_id(0)
    j = pl.program_id(1)
    tm, tc = o_ref.shape

    @pl.when(j == 0)
    def _():
        xn_ref[...] = (_rms_scale(x_ref[...]) * g_ref[...]).astype(bf16)

    xn = xn_ref[...]
    b_gate = jnp.dot(xn, wb_ref[...], preferred_element_type=f32)
    v = (jnp.dot(xn, wc_ref[...], preferred_element_type=f32)
         * jnp.dot(xn, wh_ref[...], preferred_element_type=f32))

    seq_start = lax.rem(i, tiles_per_seq) == 0
    prev = jnp.where(seq_start, lead_ref[...], carry_ref[j])
    p1 = prev[SUBLANES - 1:SUBLANES, :]
    p2 = prev[SUBLANES - 2:SUBLANES - 1, :]
    row = lax.broadcasted_iota(jnp.int32, (tm, tc), 0)
    s1 = jnp.where(row == 0, p1, pltpu.roll(v, 1, 0))
    s2 = jnp.where(row == 0, p2, jnp.where(row == 1, p1, pltpu.roll(v, 2, 0)))
    y = cw_ref[0:1, :] * s2 + cw_ref[1:2, :] * s1 + cw_ref[2:3, :] * v
    o_ref[...] = (b_gate * y).astype(o_ref.dtype)

    carry_ref[j] = v[tm - SUBLANES:, :]

    meta_v_ref[...] = v[META_BLOCK - SUBLANES:META_BLOCK, :]


def _conv_in(x2d, g, w_in, conv_w, lead, *, seq_len, tm, tc):
    rows, d = x2d.shape
    n_j = d // tc
    assert seq_len % tm == 0 and rows % tm == 0
    kern = functools.partial(_conv_in_kernel, tiles_per_seq=seq_len // tm)
    return pl.pallas_call(
        kern,
        grid=(rows // tm, n_j),
        in_specs=[
            pl.BlockSpec((tm, d), lambda i, j: (i, 0)),
            pl.BlockSpec((1, d), lambda i, j: (0, 0)),
            pl.BlockSpec((d, tc), lambda i, j: (0, j)),
            pl.BlockSpec((d, tc), lambda i, j: (0, n_j + j)),
            pl.BlockSpec((d, tc), lambda i, j: (0, 2 * n_j + j)),
            pl.BlockSpec((CONV_WIDTH, tc), lambda i, j: (0, j)),
            pl.BlockSpec((SUBLANES, tc), lambda i, j: (0, j)),
        ],
        out_specs=[pl.BlockSpec((tm, tc), lambda i, j: (i, j)),
                   pl.BlockSpec((SUBLANES, tc), lambda i, j: (i, j))],
        out_shape=[jax.ShapeDtypeStruct((rows, d), bf16),
                   jax.ShapeDtypeStruct((SUBLANES * (rows // tm), d), f32)],
        scratch_shapes=[pltpu.VMEM((tm, d), bf16), pltpu.VMEM((n_j, SUBLANES, tc), f32)],
        compiler_params=_cparams(("arbitrary", "arbitrary")),
        name="conv_in",
    )(x2d, g, w_in, w_in, w_in, conv_w, lead)


def _sb_scores(q, kb):
    return lax.dot_general(kb, q, (((1,), (1,)), ((), ())), preferred_element_type=f32)


def _sb_logs(z, mask):
    softplus_neg = jnp.log(1.0 + jnp.exp2(jnp.abs(z) * (-LOG2E)))
    log_sig = jnp.minimum(z, 0.0) - softplus_neg
    log_1m = log_sig - z
    if mask is not None:
        log_1m = jnp.where(mask, log_1m, 0.0)
    return log_sig, jnp.sum(log_1m, axis=0, keepdims=True), log_1m.astype(bf16)


def _sb_suffix_sums(u, log_1m):
    return jnp.dot(u, log_1m, preferred_element_type=f32)


def _sb_weights(log_sig, log_1m_sum, rest, carry, mask):
    a = jnp.exp2((log_sig + rest + carry) * LOG2E)
    if mask is not None:
        a = jnp.where(mask, a, 0.0)
    return a.astype(bf16), carry + log_1m_sum


def _sb_values(vb, a):
    return lax.dot_general(vb, a, (((0,), (0,)), ((), ())), preferred_element_type=f32)


def _sb_block(q, kb, vb, u, carry, mask):
    log_sig, total, log_1m = _sb_logs(_sb_scores(q, kb), mask)
    a, carry = _sb_weights(log_sig, total, _sb_suffix_sums(u, log_1m), carry, mask)
    return _sb_values(vb, a), carry


def _attn_kernel(q_ref, k_ref, v_ref, km_ref, vm_ref, u_ref, o_ref, acc_ref, carry_ref, *,
                 n_pad, n_main_steps):
    step = pl.program_id(1)
    steps_per_seq = k_ref.shape[0] // (ATT_QSUB * ATT_BLOCK)
    qstep = lax.rem(step, steps_per_seq)
    tb = ATT_BLOCK
    dh = q_ref.shape[1] // ATT_HEADS
    u = u_ref[...]
    s_loc = lax.broadcasted_iota(jnp.int32, (tb, tb), 0)
    t_loc = lax.broadcasted_iota(jnp.int32, (tb, tb), 1)
    causal = s_loc < t_loc
    meta_keys = (s_loc >= n_pad) & (s_loc < META_BLOCK)

    @pl.when(step < n_main_steps)
    def _():
        chains = [(hh, qs) for hh in range(ATT_HEADS) for qs in range(ATT_QSUB)]
        first = qstep == 0

        def keys(ref, mref, hh, qs, prev):
            cols = slice(hh * dh, (hh + 1) * dh)
            i = qstep * ATT_QSUB + qs
            blk = jnp.maximum(i - 1, 0) if prev else i
            kv = ref[pl.ds(pl.multiple_of(blk * tb, tb), tb), cols]
            if prev and qs == 0:
                kv = jnp.where(first, mref[:, cols], kv)
            return kv

        items = [(hh, qs, False) for hh, qs in chains] + [(hh, qs, True) for hh, qs in chains]
        n_items = len(items)
        masks = [(meta_keys | jnp.logical_not(first)) if (prev and qs == 0)
                 else (None if prev else causal) for hh, qs, prev in items]
        z, logs, rest, wts, contrib, carry_out = {}, {}, {}, {}, {}, {}
        for t in range(n_items + 4):
            if 0 <= t - 4 < n_items:
                n = t - 4
                hh, qs, prev = items[n]
                contrib[n] = _sb_values(keys(v_ref, vm_ref, hh, qs, prev), wts.pop(n))
            if 0 <= t - 3 < n_items:
                n = t - 3
                hh, qs, prev = items[n]
                carry_in = carry_out[n - len(chains)] if prev else jnp.zeros((1, tb), f32)
                log_sig, row0, _ = logs.pop(n)
                wts[n], carry_out[n] = _sb_weights(log_sig, row0, rest.pop(n), carry_in, masks[n])
            if 0 <= t - 2 < n_items:
                rest[t - 2] = _sb_suffix_sums(u, logs[t - 2][2])
            if 0 <= t - 1 < n_items:
                logs[t - 1] = _sb_logs(z.pop(t - 1), masks[t - 1])
            if t < n_items:
                hh, qs, prev = items[t]
                q = q_ref[qs * tb:(qs + 1) * tb, hh * dh:(hh + 1) * dh]
                z[t] = _sb_scores(q, keys(k_ref, km_ref, hh, qs, prev))
        for c, (hh, qs) in enumerate(chains):
            acc_ref[hh, qs] = contrib[c] + contrib[c + len(chains)]
            carry_ref[hh, qs] = carry_out[c + len(chains)]

        top_all = jnp.max(functools.reduce(
            jnp.maximum, [carry_out[c + len(chains)] for c in range(len(chains))]))

        @pl.when(top_all > SKIP_BELOW)
        def _():
            for hh, qs in chains:
                cols = slice(hh * dh, (hh + 1) * dh)
                i = qstep * ATT_QSUB + qs

                def cond(state):
                    j, top = state
                    return (j >= 0) & (top > SKIP_BELOW)

                def body(state, hh=hh, qs=qs, cols=cols):
                    j, _ = state
                    st = pl.multiple_of(j * tb, tb)
                    q = q_ref[qs * tb:(qs + 1) * tb, cols]
                    contrib, carry = _sb_block(q, k_ref[pl.ds(st, tb), cols],
                                               v_ref[pl.ds(st, tb), cols], u, carry_ref[hh, qs],
                                               None)
                    acc_ref[hh, qs] += contrib
                    carry_ref[hh, qs] = carry
                    return j - 1, jnp.max(carry)

                _, top = lax.while_loop(cond, body, (i - 2, jnp.max(carry_ref[hh, qs])))

                @pl.when((top > SKIP_BELOW) & (i >= 1))
                def _(hh=hh, qs=qs, cols=cols):
                    q = q_ref[qs * tb:(qs + 1) * tb, cols]
                    contrib, _ = _sb_block(q, km_ref[:, cols], vm_ref[:, cols], u,
                                           carry_ref[hh, qs], meta_keys)
                    acc_ref[hh, qs] += contrib

        for hh, qs in chains:
            o_ref[qs * tb:(qs + 1) * tb, hh * dh:(hh + 1) * dh] = acc_ref[hh, qs].T.astype(o_ref.dtype)

    @pl.when(step >= n_main_steps)
    def _():
        o_ref[...] = jnp.zeros(o_ref.shape, o_ref.dtype)
        for hh in range(ATT_HEADS):
            cols = slice(hh * dh, (hh + 1) * dh)
            contrib, _ = _sb_block(q_ref[0:tb, cols], km_ref[:, cols], vm_ref[:, cols], u,
                                   jnp.zeros((1, tb), f32), causal & meta_keys)
            o_ref[0:tb, cols] = contrib.T.astype(o_ref.dtype)


def _attention(qkv, u, *, batch, seq_len, d_model, n_pad):
    rows = qkv.shape[0]
    dh = d_model // N_HEADS
    tqs = ATT_QSUB * ATT_BLOCK
    assert tqs == TAIL_ROWS and seq_len % tqs == 0 and N_HEADS % ATT_HEADS == 0
    steps_per_seq = seq_len // tqs
    n_main = batch * steps_per_seq
    n_hp = N_HEADS // ATT_HEADS
    wcols = ATT_HEADS * dh
    meta_blk = (batch * seq_len) // ATT_BLOCK
    kern = functools.partial(_attn_kernel, n_pad=n_pad, n_main_steps=n_main)

    def seq_of(s):
        return jnp.minimum(s // steps_per_seq, batch - 1)

    return pl.pallas_call(
        kern,
        grid=(n_hp, n_main + 1),
        in_specs=[
            pl.BlockSpec((tqs, wcols), lambda h, s: (s, h)),
            pl.BlockSpec((seq_len, wcols), lambda h, s: (seq_of(s), n_hp + h)),
            pl.BlockSpec((seq_len, wcols), lambda h, s: (seq_of(s), 2 * n_hp + h)),
            pl.BlockSpec((ATT_BLOCK, wcols), lambda h, s: (meta_blk, n_hp + h)),
            pl.BlockSpec((ATT_BLOCK, wcols), lambda h, s: (meta_blk, 2 * n_hp + h)),
            pl.BlockSpec((ATT_BLOCK, ATT_BLOCK), lambda h, s: (0, 0)),
        ],
        out_specs=pl.BlockSpec((tqs, wcols), lambda h, s: (s, h)),
        out_shape=jax.ShapeDtypeStruct((rows, d_model), bf16),
        scratch_shapes=[pltpu.VMEM((ATT_HEADS, ATT_QSUB, dh, ATT_BLOCK), f32),
                        pltpu.VMEM((ATT_HEADS, ATT_QSUB, 1, ATT_BLOCK), f32)],
        compiler_params=_cparams(("arbitrary", "arbitrary")),
        name="sb_attention",
    )(qkv, qkv, qkv, qkv, qkv, u)


def _pack_rows(x, w_ref, z_ref):
    m, d = x.shape
    half = d // 2
    for c in range(half // LANES):
        z_ref[pl.ds(0, m, stride=2), :] = x[:, c * LANES:(c + 1) * LANES]
        z_ref[pl.ds(1, m, stride=2), :] = x[:, half + c * LANES:half + (c + 1) * LANES]
        words = pltpu.bitcast(z_ref[...].astype(bf16), jnp.uint32)
        w_ref[:, c, :, :] = words.reshape(m // SUBLANES, SUBLANES, LANES)


def _unpack_rows(w_ref, z_ref):
    m = w_ref.shape[0] * SUBLANES
    lo, hi = [], []
    for c in range(w_ref.shape[1]):
        words = w_ref[:, c, :, :].reshape(m, LANES)
        z_ref[...] = pltpu.bitcast(words, bf16).astype(f32)
        lo.append(z_ref[pl.ds(0, m, stride=2), :].astype(bf16))
        hi.append(z_ref[pl.ds(1, m, stride=2), :].astype(bf16))
    return jnp.concatenate(lo + hi, axis=1)


def _row_tiled(rows, width):
    return (rows // SUBLANES, width // LANES, SUBLANES, LANES)


def _row_of(ref, r):
    return ref.at[lax.shift_right_logical(r, LOG2_SUBLANES), :, pl.ds(r & (SUBLANES - 1), 1), :]


def _route_rows(h, g, w_ref, b, xp_ref, z_ref, earlier, base, n_experts):
    epg = n_experts // N_GROUPS
    xn = _rms_scale(h) * g
    _pack_rows(xn, xp_ref, z_ref)

    xh = xn.astype(bf16)
    xl = (xn - xh.astype(f32)).astype(bf16)
    hh_hl = jnp.dot(xh, w_ref[...], preferred_element_type=f32)
    logits = (hh_hl[:, :LANES] + hh_hl[:, LANES:]
              + jnp.dot(xl, w_ref[:, :LANES], preferred_element_type=f32) + b)
    lane = lax.broadcasted_iota(jnp.int32, logits.shape, 1)
    neg = jnp.float32(-jnp.inf)
    big = jnp.int32(LANES)

    gl = jnp.where(lane < N_GROUPS, logits, neg)
    gmax = jnp.max(gl, axis=-1, keepdims=True)
    g_sel = jnp.min(jnp.where(gl == gmax, lane, big), axis=-1, keepdims=True)
    p_top = 1.0 / jnp.sum(jnp.exp(gl - gmax), axis=-1, keepdims=True)

    lo_lane = N_GROUPS + g_sel * epg
    el = jnp.where((lane >= lo_lane) & (lane < lo_lane + epg), logits, neg)
    v1 = jnp.max(el, axis=-1, keepdims=True)
    l1 = jnp.min(jnp.where(el == v1, lane, big), axis=-1, keepdims=True)
    el2 = jnp.where(lane == l1, neg, el)
    v2 = jnp.max(el2, axis=-1, keepdims=True)
    l2 = jnp.min(jnp.where(el2 == v2, lane, big), axis=-1, keepdims=True)
    r = jnp.exp(v2 - v1)
    gate1 = p_top * (1.0 / (1.0 + r))
    gate2 = p_top * (r / (1.0 + r))

    onehot = ((lane == l1) | (lane == l2)).astype(f32)
    before = jnp.dot(earlier, onehot.astype(bf16), preferred_element_type=f32) + base
    rank1 = jnp.sum(jnp.where(lane == l1, before, 0.0), axis=-1, keepdims=True)
    rank2 = jnp.sum(jnp.where(lane == l2, before, 0.0), axis=-1, keepdims=True)
    base = base + jnp.sum(onehot, axis=0, keepdims=True)

    e1 = jnp.clip(l1 - N_GROUPS, 0, n_experts - 1)
    e2 = jnp.clip(l2 - N_GROUPS, 0, n_experts - 1)
    info = jnp.where(lane == 0, e1, 0)
    info = jnp.where(lane == 1, e2, info)
    info = jnp.where(lane == 2, rank1.astype(jnp.int32), info)
    info = jnp.where(lane == 3, rank2.astype(jnp.int32), info)
    info = jnp.where(lane == 4, pltpu.bitcast(jnp.broadcast_to(gate1, logits.shape), jnp.int32), info)
    info = jnp.where(lane == 5, pltpu.bitcast(jnp.broadcast_to(gate2, logits.shape), jnp.int32), info)
    return info, base


def _proj_router_kernel(a_ref, at_ref, w_ref, r_ref, rt_ref, g_ref, wr_ref, br_ref, h_ref, xp_ref,
                        info_ref, infot_ref, cnt_ref, base_ref, z_ref, *, n_experts, n_main,
                        n_sub):
    i = pl.program_id(0)
    tm = h_ref.shape[0]
    ts = tm // n_sub

    @pl.when(i == 0)
    def _():
        base_ref[...] = jnp.zeros(base_ref.shape, f32)

    hs = []
    for s in range(n_sub):
        rows = slice(s * ts, (s + 1) * ts)
        a, r = a_ref[rows, :], r_ref[rows, :]
        if at_ref is not None:
            a = jnp.where(i >= n_main, at_ref[rows, :], a)
            r = jnp.where(i >= n_main, rt_ref[rows, :], r)
        h = r + jnp.dot(a, w_ref[...], preferred_element_type=f32)
        h_ref[rows, :] = h
        hs.append(h)

    r_i = lax.broadcasted_iota(jnp.int32, (ts, ts), 0)
    c_i = lax.broadcasted_iota(jnp.int32, (ts, ts), 1)
    earlier = jnp.where(c_i < r_i, 1.0, 0.0).astype(bf16)
    base = base_ref[...]
    for s, h in enumerate(hs):
        xp_rows = xp_ref.at[pl.ds(s * (ts // SUBLANES), ts // SUBLANES)]
        info, base = _route_rows(h, g_ref[...], wr_ref, br_ref[...], xp_rows, z_ref, earlier, base,
                                 n_experts)
        info_ref[s * ts:(s + 1) * ts, :] = info
        infot_ref[:, s * ts:(s + 1) * ts] = info.T[0:INFO_ROWS, :]
    base_ref[...] = base
    cnt_ref[...] = base.astype(jnp.int32)


def _proj_router(a, a_tail, w, r, r_tail, g, w_r, b_r, *, n_experts, tm, n_sub):
    k = a.shape[1]
    d = w.shape[1]
    n_main = a.shape[0] // tm
    has_tail = a_tail is not None
    rows = a.shape[0] + (tm if has_tail else 0)
    assert a.shape[0] % tm == 0 and r.shape[0] == a.shape[0]

    def main_blk(i):
        return (jnp.minimum(i, n_main - 1), 0)

    def first_blk(i):
        return (0, 0)

    def const(shape):
        return pl.BlockSpec(shape, lambda i: (0,) * len(shape))

    kern = functools.partial(_proj_router_kernel, n_experts=n_experts, n_main=n_main, n_sub=n_sub)
    if has_tail:
        assert a_tail.shape[0] == tm and r_tail.shape[0] == tm
        operands = (a, a_tail, w, r, r_tail, g, w_r, b_r)
        tail_specs = [pl.BlockSpec((tm, k), first_blk), pl.BlockSpec((tm, d), first_blk)]
    else:
        operands = (a, w, r, g, w_r, b_r)
        tail_specs = [None, None]
        kern = functools.partial(_no_tail, kern)
    in_specs = [pl.BlockSpec((tm, k), main_blk), tail_specs[0], const((k, d)),
                pl.BlockSpec((tm, d), main_blk), tail_specs[1], const((1, d)),
                const((d, 2 * LANES)), const((1, LANES))]
    return pl.pallas_call(
        kern,
        grid=(rows // tm,),
        in_specs=[spec for spec in in_specs if spec is not None],
        out_specs=[
            pl.BlockSpec((tm, d), lambda i: (i, 0)),
            pl.BlockSpec(_row_tiled(tm, d // 2), lambda i: (i, 0, 0, 0)),
            pl.BlockSpec((tm, LANES), lambda i: (i, 0)),
            pl.BlockSpec((INFO_ROWS, tm), lambda i: (0, i)),
            const((1, LANES)),
        ],
        out_shape=[
            jax.ShapeDtypeStruct((rows, d), f32),
            jax.ShapeDtypeStruct(_row_tiled(rows, d // 2), jnp.uint32),
            jax.ShapeDtypeStruct((rows, LANES), jnp.int32),
            jax.ShapeDtypeStruct((INFO_ROWS, rows), jnp.int32),
            jax.ShapeDtypeStruct((1, LANES), jnp.int32),
        ],
        scratch_shapes=[pltpu.VMEM((1, LANES), f32), pltpu.VMEM((2 * tm // n_sub, LANES), f32)],
        compiler_params=_cparams(("arbitrary",)),
        name="out_proj_router",
    )(*operands)


def _no_tail(kern, a_ref, w_ref, r_ref, *rest):
    return kern(a_ref, None, w_ref, r_ref, None, *rest)


def _pad_fill_copies(fs_ref, pad_ref, zbuf, xs_ref, sem, e):
    fs = fs_ref[e]
    pad = pad_ref[e]
    head = pad & (SUBLANES - 1)
    out = []
    for k in range(SUBLANES - 1):
        out.append((k < head,
                    pltpu.make_async_copy(zbuf.at[0, :, pl.ds(0, 1), :], _row_of(xs_ref, fs + k), sem)))
    cur = lax.shift_right_logical(fs + head, LOG2_SUBLANES)
    for b in reversed(range(SUBLANES.bit_length() - 1, MOE_TILE.bit_length() - 1)):
        size = 1 << b
        groups = size // SUBLANES
        out.append(((pad & size) != 0,
                    pltpu.make_async_copy(zbuf.at[pl.ds(0, groups)], xs_ref.at[pl.ds(cur, groups)], sem)))
        cur = cur + lax.shift_right_logical(pad & size, LOG2_SUBLANES)
    return out


def _unused_tile_copies(zbuf, xs_ref, sem, t):
    groups = zbuf.shape[0]
    per_tile = MOE_TILE // SUBLANES
    return [pltpu.make_async_copy(zbuf, xs_ref.at[pl.ds(t * per_tile + k * groups, groups)], sem)
            for k in range(per_tile // groups)]


def _dispatch_kernel(p0_ref, p1_ref, fs_ref, pad_ref, nu_ref, x_ref, xs_ref, zbuf, sem, zsem, *,
                     n_steps, n_experts, max_tiles):
    i = pl.program_id(0)
    tm = x_ref.shape[0] * SUBLANES
    base = i * tm

    @pl.when(i == 0)
    def _():
        zbuf[...] = jnp.zeros(zbuf.shape, zbuf.dtype)

        def fill(e, c):
            for needed, copy in _pad_fill_copies(fs_ref, pad_ref, zbuf, xs_ref, zsem, e):
                @pl.when(needed)
                def _():
                    copy.start()
            return c

        lax.fori_loop(0, n_experts, fill, 0)

        def fill_unused(t, c):
            for copy in _unused_tile_copies(zbuf, xs_ref, zsem, t):
                copy.start()
            return c

        lax.fori_loop(nu_ref[0], max_tiles, fill_unused, 0)

    def issue(g, c):
        for k in range(SUBLANES):
            src = x_ref.at[g, :, pl.ds(k, 1), :]
            r = base + g * SUBLANES + k
            pltpu.make_async_copy(src, _row_of(xs_ref, p0_ref[r]), sem).start(priority=0)
            pltpu.make_async_copy(src, _row_of(xs_ref, p1_ref[r]), sem).start(priority=1)
        return c

    lax.fori_loop(0, tm // SUBLANES, issue, 0)
    for _ in range(TOP_K):
        pltpu.make_async_copy(x_ref, xs_ref.at[pl.ds(0, tm // SUBLANES)], sem).wait()

    @pl.when(i == n_steps - 1)
    def _():
        def drain(e, c):
            for needed, copy in _pad_fill_copies(fs_ref, pad_ref, zbuf, xs_ref, zsem, e):
                @pl.when(needed)
                def _():
                    copy.wait()
            return c

        lax.fori_loop(0, n_experts, drain, 0)

        def drain_unused(t, c):
            for copy in _unused_tile_copies(zbuf, xs_ref, zsem, t):
                copy.wait()
            return c

        lax.fori_loop(nu_ref[0], max_tiles, drain_unused, 0)


def _dispatch(pos0, pos1, fill_start, pad, n_used, xp, *, sorted_rows, tm):
    width = xp.shape[1] * LANES
    n_steps = (xp.shape[0] * SUBLANES) // tm
    kern = functools.partial(_dispatch_kernel, n_steps=n_steps, n_experts=pad.shape[0],
                             max_tiles=sorted_rows // MOE_TILE)
    return pl.pallas_call(
        kern,
        grid_spec=pltpu.PrefetchScalarGridSpec(
            num_scalar_prefetch=5,
            grid=(n_steps,),
            in_specs=[pl.BlockSpec(_row_tiled(tm, width), lambda i, *_: (i, 0, 0, 0))],
            out_specs=pl.BlockSpec(memory_space=pl.ANY),
            scratch_shapes=[pltpu.VMEM(_row_tiled(MOE_TILE // 2, width), jnp.uint32),
                            pltpu.SemaphoreType.DMA(()), pltpu.SemaphoreType.DMA(())],
        ),
        out_shape=jax.ShapeDtypeStruct(_row_tiled(sorted_rows, width), jnp.uint32),
        compiler_params=pltpu.CompilerParams(dimension_semantics=("arbitrary",),
                                             vmem_limit_bytes=VMEM_LIMIT, has_side_effects=True),
        name="moe_dispatch",
    )(pos0, pos1, fill_start, pad, n_used, xp)


def _experts_kernel(te_ref, tf_ref, nx_ref, nu_ref, xs_ref, wg_hbm, wu_hbm, wd_hbm, y_ref,
                    wg_f, wu_f, wd_f, wg_b, wu_b, wd_b, z_ref, slot_ref, sems, *, layer):
    t = pl.program_id(0)

    def weight_copies(e, slot):
        return [pltpu.make_async_copy(hbm.at[layer, e], buf.at[slot], sems.at[slot, k])
                for k, (hbm, buf) in enumerate(((wg_hbm, wg_f), (wu_hbm, wu_f), (wd_hbm, wd_f)))]

    @pl.when(t == 0)
    def _():
        slot_ref[0] = 0
        for c in weight_copies(te_ref[0], 0):
            c.start()

    @pl.when(tf_ref[t] == 1)
    def _():
        slot = slot_ref[0]
        for c in weight_copies(te_ref[t], slot):
            c.wait()

        @pl.when(nx_ref[t] >= 0)
        def _():
            for c in weight_copies(nx_ref[t], 1 - slot):
                c.start()

        wg_b[...] = wg_f[slot].astype(bf16)
        wu_b[...] = wu_f[slot].astype(bf16)
        wd_b[...] = wd_f[slot].astype(bf16)
        slot_ref[0] = 1 - slot

    @pl.when(t < nu_ref[0])
    def _():
        x = _unpack_rows(xs_ref, z_ref)
        gate = jnp.dot(x, wg_b[...], preferred_element_type=f32)
        up = jnp.dot(x, wu_b[...], preferred_element_type=f32)
        hidden = (gate * (1.0 / (1.0 + jnp.exp(-gate))) * up).astype(bf16)
        _pack_rows(jnp.dot(hidden, wd_b[...], preferred_element_type=f32), y_ref, z_ref)

    @pl.when(t >= nu_ref[0])
    def _():
        y_ref[...] = jnp.zeros(y_ref.shape, y_ref.dtype)


def _experts(tile_expert, tile_first, next_expert, n_used, xs, w_gate, w_up, w_down, *, layer,
             max_tiles):
    tm = MOE_TILE
    _, _, d, f = w_gate.shape
    kern = functools.partial(_experts_kernel, layer=layer)
    hbm = pl.BlockSpec(memory_space=pl.ANY)
    return pl.pallas_call(
        kern,
        grid_spec=pltpu.PrefetchScalarGridSpec(
            num_scalar_prefetch=4,
            grid=(max_tiles,),
            in_specs=[pl.BlockSpec(_row_tiled(tm, d // 2),
                                   lambda t, te, tf, nx, nu: (jnp.clip(nu[0] - 1, 0, t), 0, 0, 0)),
                      hbm, hbm, hbm],
            out_specs=pl.BlockSpec(_row_tiled(tm, d // 2), lambda t, *_: (t, 0, 0, 0)),
            scratch_shapes=[pltpu.VMEM((2, d, f), f32), pltpu.VMEM((2, d, f), f32),
                            pltpu.VMEM((2, f, d), f32),
                            pltpu.VMEM((d, f), bf16), pltpu.VMEM((d, f), bf16),
                            pltpu.VMEM((f, d), bf16), pltpu.VMEM((2 * tm, LANES), f32),
                            pltpu.SMEM((1,), jnp.int32), pltpu.SemaphoreType.DMA((2, 3))],
        ),
        out_shape=jax.ShapeDtypeStruct(_row_tiled(max_tiles * tm, d // 2), jnp.uint32),
        compiler_params=_cparams(("arbitrary",)),
        name="moe_experts",
    )(tile_expert, tile_first, next_expert, n_used, xs, w_gate, w_up, w_down)


def _combine_kernel(p0_ref, p1_ref, h_ref, info_ref, gf_ref, ys_ref, o_ref, buf, z_ref, sems, *,
                    n_steps, final_norm):
    i = pl.program_id(0)
    tm = h_ref.shape[0]

    def issue(step, slot):
        base = step * tm

        def body(g, c):
            for k in range(SUBLANES):
                r = base + g * SUBLANES + k
                pltpu.make_async_copy(_row_of(ys_ref, p0_ref[r]),
                                      buf.at[slot, 0, g, :, pl.ds(k, 1), :],
                                      sems.at[slot]).start(priority=0)
                pltpu.make_async_copy(_row_of(ys_ref, p1_ref[r]),
                                      buf.at[slot, 1, g, :, pl.ds(k, 1), :],
                                      sems.at[slot]).start(priority=1)
            return c

        lax.fori_loop(0, tm // SUBLANES, body, 0)

    @pl.when(i == 0)
    def _():
        issue(0, 0)

    @pl.when(i + 1 < n_steps)
    def _():
        issue(i + 1, lax.rem(i + 1, 2))

    slot = lax.rem(i, 2)
    for k in range(TOP_K):
        pltpu.make_async_copy(ys_ref.at[pl.ds(0, tm // SUBLANES)], buf.at[slot, k],
                              sems.at[slot]).wait()

    info = info_ref[...]
    lane = lax.broadcasted_iota(jnp.int32, info.shape, 1)
    gates = pltpu.bitcast(info, f32)
    g1 = jnp.sum(jnp.where(lane == 4, gates, 0.0), axis=-1, keepdims=True)
    g2 = jnp.sum(jnp.where(lane == 5, gates, 0.0), axis=-1, keepdims=True)
    lo, hi = [], []
    for c in range(buf.shape[3]):
        for k in range(TOP_K):
            words = buf[slot, k, :, c, :, :].reshape(tm, LANES)
            z_ref[k] = pltpu.bitcast(words, bf16).astype(f32)
        lo.append(g1 * z_ref[0, pl.ds(0, tm, stride=2), :] + g2 * z_ref[1, pl.ds(0, tm, stride=2), :])
        hi.append(g1 * z_ref[0, pl.ds(1, tm, stride=2), :] + g2 * z_ref[1, pl.ds(1, tm, stride=2), :])
    out = h_ref[...] + jnp.concatenate(lo + hi, axis=1)
    if final_norm:
        out = _rms_scale(out) * gf_ref[...]
    o_ref[...] = out


def _combine(pos0, pos1, h, info, gf, ys, *, out_rows, tm, final_norm):
    d = h.shape[1]
    n_steps = out_rows // tm
    kern = functools.partial(_combine_kernel, n_steps=n_steps, final_norm=final_norm)
    return pl.pallas_call(
        kern,
        grid_spec=pltpu.PrefetchScalarGridSpec(
            num_scalar_prefetch=2,
            grid=(n_steps,),
            in_specs=[
                pl.BlockSpec((tm, d), lambda i, p0, p1: (i, 0)),
                pl.BlockSpec((tm, LANES), lambda i, p0, p1: (i, 0)),
                pl.BlockSpec((1, d), lambda i, p0, p1: (0, 0)),
                pl.BlockSpec(memory_space=pl.ANY),
            ],
            out_specs=pl.BlockSpec((tm, d), lambda i, p0, p1: (i, 0)),
            scratch_shapes=[pltpu.VMEM((2, TOP_K) + _row_tiled(tm, d // 2), jnp.uint32),
                            pltpu.VMEM((TOP_K, 2 * tm, LANES), f32),
                            pltpu.SemaphoreType.DMA((2,))],
        ),
        out_shape=jax.ShapeDtypeStruct((out_rows, d), f32),
        compiler_params=_cparams(("arbitrary",)),
        name="moe_combine",
    )(pos0, pos1, h, info, gf, ys)


def _table_lookup(table, idx):
    k = lax.broadcasted_iota(jnp.int32, (table.shape[0], idx.shape[0]), 0)
    return jnp.sum(jnp.where(k == idx[None, :], table[:, None], 0), axis=0)


def _router_params(wg_r, bg_r, we_r, be_r):
    d, n_experts = we_r.shape
    n_fill = LANES - N_GROUPS - n_experts
    w_r = jnp.concatenate([wg_r, we_r, jnp.zeros((d, n_fill), f32)], axis=1)
    b_r = jnp.concatenate([bg_r, be_r, jnp.zeros((n_fill,), f32)]).reshape(1, LANES)
    w_hi = w_r.astype(bf16)
    return jnp.concatenate([w_hi, (w_r - w_hi.astype(f32)).astype(bf16)], axis=1), b_r


def _mixer_out_and_moe(a, a_tail, w_out, r, r_tail, g_ffn, wg_r, bg_r, we_r, be_r, w_gate, w_up,
                       w_down, gf, *, layer, out_rows, final_norm):
    d = w_out.shape[1]
    n_experts = we_r.shape[1]
    w_hl, b_r = _router_params(wg_r, bg_r, we_r, be_r)
    h, xp, info, info_t, cnt = _proj_router(a, a_tail, w_out, r, r_tail, g_ffn.reshape(1, d), w_hl,
                                            b_r, n_experts=n_experts, tm=TAIL_ROWS, n_sub=2)
    rows = h.shape[0]

    tm = MOE_TILE
    max_tiles = (TOP_K * rows) // tm + n_experts
    i32 = jnp.int32
    cnt = cnt[0, N_GROUPS:N_GROUPS + n_experts]
    n_tiles = (cnt + tm - 1) // tm
    tile_end = jnp.cumsum(n_tiles)
    tile_start = tile_end - n_tiles
    n_used = tile_end[-1]
    t_ids = jnp.arange(max_tiles, dtype=i32)[None, :]
    e_ids = jnp.arange(n_experts, dtype=i32)[:, None]
    first_t, end_t = tile_start[:, None], tile_end[:, None]
    member = (first_t <= t_ids) & (t_ids < end_t)
    te = jnp.sum(jnp.where(member, e_ids, 0), axis=0)
    tile_first = jnp.any(member & (first_t == t_ids), axis=0).astype(i32)
    nxt_tile = jnp.sum(jnp.where(member, end_t, 0), axis=0)[None, :]
    is_next = (first_t == nxt_tile) & (n_tiles[:, None] > 0)
    next_expert = jnp.where(jnp.any(is_next, axis=0), jnp.sum(jnp.where(is_next, e_ids, 0), axis=0), -1)

    sorted_rows = max_tiles * tm
    row_off = (tile_start * tm).astype(i32)
    pos0 = jnp.clip(_table_lookup(row_off, info_t[0]) + info_t[2], 0, sorted_rows - 1)
    pos1 = jnp.clip(_table_lookup(row_off, info_t[1]) + info_t[3], 0, sorted_rows - 1)

    n_used = n_used.reshape(1).astype(i32)
    xs = _dispatch(pos0, pos1, (row_off + cnt).astype(i32), (n_tiles * tm - cnt).astype(i32), n_used,
                   xp, sorted_rows=sorted_rows, tm=512)
    ys = _experts(te.astype(i32), tile_first, next_expert.astype(i32), n_used,
                  xs, w_gate, w_up, w_down, layer=layer, max_tiles=max_tiles)
    return _combine(pos0, pos1, h, info, gf.reshape(1, d), ys, out_rows=out_rows, tm=512,
                    final_norm=final_norm)


def _strict_upper_ones(n):
    r = lax.broadcasted_iota(jnp.int32, (n, n), 0)
    c = lax.broadcasted_iota(jnp.int32, (n, n), 1)
    return (c > r).astype(bf16)


def kernel(x, meta_tokens, norm_mix_g, norm_ffn_g, conv_w_in, conv_w, conv_w_out, attn_w_qkv, attn_w_out, router_group_w, router_group_b, router_expert_w, router_expert_b, moe_w_gate, moe_w_up, moe_w_down, final_norm_g):
    batch, seq_len, d = x.shape
    n_meta = meta_tokens.shape[0]
    n_pad = META_BLOCK - n_meta
    n_tok = batch * seq_len
    assert seq_len % TAIL_ROWS == 0 and d % (2 * N_HEADS) == 0

    x2d = x.reshape(n_tok, d)
    tail = jnp.concatenate([jnp.zeros((n_pad, d), f32), meta_tokens.astype(f32),
                            jnp.zeros((TAIL_ROWS - META_BLOCK, d), f32)], axis=0)

    tc = _largest_divisor(d, (512, 256, 128))
    g_mix0 = norm_mix_g[0].reshape(1, d)
    w_in = conv_w_in[0].astype(bf16)
    gated_tail, meta_v = _conv_in(tail, g_mix0, w_in, conv_w[0], jnp.zeros((SUBLANES, d), f32),
                                  seq_len=TAIL_ROWS, tm=TAIL_ROWS, tc=tc)
    gated, _ = _conv_in(x2d, g_mix0, w_in, conv_w[0], meta_v, seq_len=seq_len,
                        tm=_largest_divisor(seq_len, (1024, 512)), tc=tc)
    h = _mixer_out_and_moe(gated, gated_tail, conv_w_out[0].astype(bf16), x2d, tail, norm_ffn_g[0],
                           router_group_w[0], router_group_b[0], router_expert_w[0],
                           router_expert_b[0], moe_w_gate, moe_w_up, moe_w_down, final_norm_g,
                           layer=0, out_rows=n_tok + TAIL_ROWS, final_norm=False)

    qkv = _norm_matmul(h, norm_mix_g[1].reshape(1, d), attn_w_qkv[0].astype(bf16),
                       tm=_largest_divisor(h.shape[0], (1536, 512)),
                       tn=_largest_divisor(d, (1024, 512, 256, 128)),
                       scaled_cols=d, scale=(d // N_HEADS) ** -0.5)
    o = _attention(qkv, _strict_upper_ones(ATT_BLOCK), batch=batch, seq_len=seq_len, d_model=d,
                   n_pad=n_pad)
    out = _mixer_out_and_moe(o, None, attn_w_out[0].astype(bf16), h, None, norm_ffn_g[1],
                             router_group_w[1], router_group_b[1], router_expert_w[1],
                             router_expert_b[1], moe_w_gate, moe_w_up, moe_w_down, final_norm_g,
                             layer=1, out_rows=n_tok, final_norm=True)
    return out.reshape(batch, seq_len, d)
```

```python
import functools

import jax
import jax.numpy as jnp
from jax import lax
from jax.experimental import pallas as pl
from jax.experimental.pallas import tpu as pltpu

N_HEADS = 16
N_GROUPS = 4
TOP_K = 2
CONV_WIDTH = 3
EPS = 1e-6

LANES = 128
SUBLANES = 8
LOG2_SUBLANES = SUBLANES.bit_length() - 1
META_BLOCK = 128
TAIL_ROWS = 512
MOE_TILE = 256
ATT_BLOCK = 256
ATT_HEADS = 8
ATT_QSUB = 2
INFO_ROWS = 8
CONV_ROW_TILES = (1024, 512)
CONV_COL_TILES = (512, 256)
QKV_ROW_TILES = (1536, 512)
QKV_COL_TILES = (1024, 512, 256, 128)
DISPATCH_ROWS = 512
COMBINE_ROWS = 256
SKIP_BELOW = -104.0
LOG2E = 1.4426950408889634
VMEM_LIMIT = 56 * 1024 * 1024

f32 = jnp.float32
bf16 = jnp.bfloat16


def _cparams(sem):
    return pltpu.CompilerParams(dimension_semantics=sem, vmem_limit_bytes=VMEM_LIMIT)


def _largest_divisor(n, candidates):
    return next(c for c in candidates if n % c == 0)


def _rms_scale(x):
    return x * lax.rsqrt(jnp.mean(x * x, axis=-1, keepdims=True) + EPS)


def _norm_matmul_kernel(h_ref, g_ref, w_ref, o_ref, xn_ref, *, n_scaled, scale):
    j = pl.program_id(1)

    @pl.when(j == 0)
    def _():
        xn_ref[...] = (_rms_scale(h_ref[...]) * g_ref[...]).astype(bf16)

    mult = jnp.where(j < n_scaled, scale, 1.0)
    o_ref[...] = (jnp.dot(xn_ref[...], w_ref[...], preferred_element_type=f32) * mult).astype(o_ref.dtype)


def _norm_matmul(h, g, w, *, tm, tn, scaled_cols, scale):
    rows, d = h.shape
    n = w.shape[1]
    assert scaled_cols % tn == 0
    kern = functools.partial(_norm_matmul_kernel, n_scaled=scaled_cols // tn, scale=scale)
    return pl.pallas_call(
        kern,
        grid=(rows // tm, n // tn),
        in_specs=[
            pl.BlockSpec((tm, d), lambda i, j: (i, 0)),
            pl.BlockSpec((1, d), lambda i, j: (0, 0)),
            pl.BlockSpec((d, tn), lambda i, j: (0, j)),
        ],
        out_specs=pl.BlockSpec((tm, tn), lambda i, j: (i, j)),
        out_shape=jax.ShapeDtypeStruct((rows, n), bf16),
        scratch_shapes=[pltpu.VMEM((tm, d), bf16)],
        compiler_params=_cparams(("arbitrary", "arbitrary")),
        name="norm_qkv",
    )(h, g, w)


def _conv_in_kernel(x_ref, g_ref, wb_ref, wc_ref, wh_ref, cw_ref, lead_ref, o_ref, meta_v_ref,
                    xn_ref, carry_ref, *, tiles_per_seq):
    i = pl.program_id(0)
    j = pl.program_id(1)
    tm, tc = o_ref.shape

    @pl.when(j == 0)
    def _():
        xn_ref[...] = (_rms_scale(x_ref[...]) * g_ref[...]).astype(bf16)

    xn = xn_ref[...]
    seq_start = lax.rem(i, tiles_per_seq) == 0
    row = lax.broadcasted_iota(jnp.int32, (tm, tc // 2), 0)
    halves = (slice(0, tc // 2), slice(tc // 2, tc))
    projected = [tuple(jnp.dot(xn, w[:, cols], preferred_element_type=f32)
                       for w in (wb_ref, wc_ref, wh_ref)) for cols in halves]
    for cols, (b_gate, c_gate, hh) in zip(halves, projected):
        v = c_gate * hh
        prev = jnp.where(seq_start, lead_ref[:, cols], carry_ref[j, :, cols])
        p1 = prev[SUBLANES - 1:SUBLANES, :]
        p2 = prev[SUBLANES - 2:SUBLANES - 1, :]
        s1 = jnp.where(row == 0, p1, pltpu.roll(v, 1, 0))
        s2 = jnp.where(row == 0, p2, jnp.where(row == 1, p1, pltpu.roll(v, 2, 0)))
        y = cw_ref[0:1, cols] * s2 + cw_ref[1:2, cols] * s1 + cw_ref[2:3, cols] * v
        o_ref[:, cols] = (b_gate * y).astype(o_ref.dtype)
        carry_ref[j, :, cols] = v[tm - SUBLANES:, :]
        meta_v_ref[:, cols] = v[META_BLOCK - SUBLANES:META_BLOCK, :]


def _conv_in(x2d, g, w_in, conv_w, lead, *, seq_len, tm, tc):
    rows, d = x2d.shape
    n_j = d // tc
    assert seq_len % tm == 0 and rows % tm == 0
    kern = functools.partial(_conv_in_kernel, tiles_per_seq=seq_len // tm)
    return pl.pallas_call(
        kern,
        grid=(rows // tm, n_j),
        in_specs=[
            pl.BlockSpec((tm, d), lambda i, j: (i, 0)),
            pl.BlockSpec((1, d), lambda i, j: (0, 0)),
            pl.BlockSpec((d, tc), lambda i, j: (0, j)),
            pl.BlockSpec((d, tc), lambda i, j: (0, n_j + j)),
            pl.BlockSpec((d, tc), lambda i, j: (0, 2 * n_j + j)),
            pl.BlockSpec((CONV_WIDTH, tc), lambda i, j: (0, j)),
            pl.BlockSpec((SUBLANES, tc), lambda i, j: (0, j)),
        ],
        out_specs=[pl.BlockSpec((tm, tc), lambda i, j: (i, j)),
                   pl.BlockSpec((SUBLANES, tc), lambda i, j: (i, j))],
        out_shape=[jax.ShapeDtypeStruct((rows, d), bf16),
                   jax.ShapeDtypeStruct((SUBLANES * (rows // tm), d), f32)],
        scratch_shapes=[pltpu.VMEM((tm, d), bf16), pltpu.VMEM((n_j, SUBLANES, tc), f32)],
        compiler_params=_cparams(("arbitrary", "arbitrary")),
        name="conv_in",
    )(x2d, g, w_in, w_in, w_in, conv_w, lead)


def _sb_scores(q, kb):
    return lax.dot_general(kb, q, (((1,), (1,)), ((), ())), preferred_element_type=f32)


def _sb_logs(z, mask):
    softplus_neg = jnp.log(1.0 + jnp.exp2(jnp.abs(z) * (-LOG2E)))
    log_sig = jnp.minimum(z, 0.0) - softplus_neg
    log_1m = log_sig - z
    if mask is not None:
        log_1m = jnp.where(mask, log_1m, 0.0)
    return log_sig, jnp.sum(log_1m, axis=0, keepdims=True), log_1m.astype(bf16)


def _sb_suffix_sums(u, log_1m):
    return jnp.dot(u, log_1m, preferred_element_type=f32)


def _sb_weights(log_sig, log_1m_sum, rest, carry, mask):
    a = jnp.exp2((log_sig + rest + carry) * LOG2E)
    if mask is not None:
        a = jnp.where(mask, a, 0.0)
    return a.astype(bf16), carry + log_1m_sum


def _sb_values(vb, a):
    return lax.dot_general(vb, a, (((0,), (0,)), ((), ())), preferred_element_type=f32)


def _sb_block(q, kb, vb, u, carry, mask):
    log_sig, total, log_1m = _sb_logs(_sb_scores(q, kb), mask)
    a, carry = _sb_weights(log_sig, total, _sb_suffix_sums(u, log_1m), carry, mask)
    return _sb_values(vb, a), carry


def _attn_kernel(q_ref, k_ref, v_ref, km_ref, vm_ref, u_ref, o_ref, acc_ref, carry_ref, *,
                 n_pad, n_main_steps):
    step = pl.program_id(1)
    steps_per_seq = k_ref.shape[0] // (ATT_QSUB * ATT_BLOCK)
    qstep = lax.rem(step, steps_per_seq)
    tb = ATT_BLOCK
    dh = q_ref.shape[1] // ATT_HEADS
    u = u_ref[...]
    s_loc = lax.broadcasted_iota(jnp.int32, (tb, tb), 0)
    t_loc = lax.broadcasted_iota(jnp.int32, (tb, tb), 1)
    causal = s_loc < t_loc
    meta_keys = (s_loc >= n_pad) & (s_loc < META_BLOCK)

    @pl.when(step < n_main_steps)
    def _():
        chains = [(hh, qs) for hh in range(ATT_HEADS) for qs in range(ATT_QSUB)]
        first = qstep == 0

        def keys(ref, mref, hh, qs, prev):
            cols = slice(hh * dh, (hh + 1) * dh)
            i = qstep * ATT_QSUB + qs
            blk = jnp.maximum(i - 1, 0) if prev else i
            kv = ref[pl.ds(pl.multiple_of(blk * tb, tb), tb), cols]
            if prev and qs == 0:
                kv = jnp.where(first, mref[:, cols], kv)
            return kv

        items = [(hh, qs, False) for hh, qs in chains] + [(hh, qs, True) for hh, qs in chains]
        n_items = len(items)
        masks = [(meta_keys | jnp.logical_not(first)) if (prev and qs == 0)
                 else (None if prev else causal) for hh, qs, prev in items]
        z, logs, rest, wts, contrib, carry_out = {}, {}, {}, {}, {}, {}
        for t in range(n_items + 4):
            if 0 <= t - 4 < n_items:
                n = t - 4
                hh, qs, prev = items[n]
                contrib[n] = _sb_values(keys(v_ref, vm_ref, hh, qs, prev), wts.pop(n))
            if 0 <= t - 3 < n_items:
                n = t - 3
                hh, qs, prev = items[n]
                carry_in = carry_out[n - len(chains)] if prev else jnp.zeros((1, tb), f32)
                log_sig, row0, _ = logs.pop(n)
                wts[n], carry_out[n] = _sb_weights(log_sig, row0, rest.pop(n), carry_in, masks[n])
            if 0 <= t - 2 < n_items:
                rest[t - 2] = _sb_suffix_sums(u, logs[t - 2][2])
            if 0 <= t - 1 < n_items:
                logs[t - 1] = _sb_logs(z.pop(t - 1), masks[t - 1])
            if t < n_items:
                hh, qs, prev = items[t]
                q = q_ref[qs * tb:(qs + 1) * tb, hh * dh:(hh + 1) * dh]
                z[t] = _sb_scores(q, keys(k_ref, km_ref, hh, qs, prev))
        for c, (hh, qs) in enumerate(chains):
            acc_ref[hh, qs] = contrib[c] + contrib[c + len(chains)]
            carry_ref[hh, qs] = carry_out[c + len(chains)]

        top_all = jnp.max(functools.reduce(
            jnp.maximum, [carry_out[c + len(chains)] for c in range(len(chains))]))

        @pl.when(top_all > SKIP_BELOW)
        def _():
            for hh, qs in chains:
                cols = slice(hh * dh, (hh + 1) * dh)
                i = qstep * ATT_QSUB + qs

                def cond(state):
                    j, top = state
                    return (j >= 0) & (top > SKIP_BELOW)

                def body(state, hh=hh, qs=qs, cols=cols):
                    j, _ = state
                    st = pl.multiple_of(j * tb, tb)
                    q = q_ref[qs * tb:(qs + 1) * tb, cols]
                    contrib, carry = _sb_block(q, k_ref[pl.ds(st, tb), cols],
                                               v_ref[pl.ds(st, tb), cols], u, carry_ref[hh, qs],
                                               None)
                    acc_ref[hh, qs] += contrib
                    carry_ref[hh, qs] = carry
                    return j - 1, jnp.max(carry)

                _, top = lax.while_loop(cond, body, (i - 2, jnp.max(carry_ref[hh, qs])))

                @pl.when((top > SKIP_BELOW) & (i >= 1))
                def _(hh=hh, qs=qs, cols=cols):
                    q = q_ref[qs * tb:(qs + 1) * tb, cols]
                    contrib, _ = _sb_block(q, km_ref[:, cols], vm_ref[:, cols], u,
                                           carry_ref[hh, qs], meta_keys)
                    acc_ref[hh, qs] += contrib

        for hh, qs in chains:
            o_ref[qs * tb:(qs + 1) * tb, hh * dh:(hh + 1) * dh] = acc_ref[hh, qs].T.astype(o_ref.dtype)

    @pl.when(step >= n_main_steps)
    def _():
        o_ref[...] = jnp.zeros(o_ref.shape, o_ref.dtype)
        for hh in range(ATT_HEADS):
            cols = slice(hh * dh, (hh + 1) * dh)
            contrib, _ = _sb_block(q_ref[0:tb, cols], km_ref[:, cols], vm_ref[:, cols], u,
                                   jnp.zeros((1, tb), f32), causal & meta_keys)
            o_ref[0:tb, cols] = contrib.T.astype(o_ref.dtype)


def _attention(qkv, u, *, batch, seq_len, d_model, n_pad):
    rows = qkv.shape[0]
    dh = d_model // N_HEADS
    tqs = ATT_QSUB * ATT_BLOCK
    assert tqs == TAIL_ROWS and seq_len % tqs == 0 and N_HEADS % ATT_HEADS == 0
    steps_per_seq = seq_len // tqs
    n_main = batch * steps_per_seq
    n_hp = N_HEADS // ATT_HEADS
    wcols = ATT_HEADS * dh
    meta_blk = (batch * seq_len) // ATT_BLOCK
    kern = functools.partial(_attn_kernel, n_pad=n_pad, n_main_steps=n_main)

    def seq_of(s):
        return jnp.minimum(s // steps_per_seq, batch - 1)

    return pl.pallas_call(
        kern,
        grid=(n_hp, n_main + 1),
        in_specs=[
            pl.BlockSpec((tqs, wcols), lambda h, s: (s, h)),
            pl.BlockSpec((seq_len, wcols), lambda h, s: (seq_of(s), n_hp + h)),
            pl.BlockSpec((seq_len, wcols), lambda h, s: (seq_of(s), 2 * n_hp + h)),
            pl.BlockSpec((ATT_BLOCK, wcols), lambda h, s: (meta_blk, n_hp + h)),
            pl.BlockSpec((ATT_BLOCK, wcols), lambda h, s: (meta_blk, 2 * n_hp + h)),
            pl.BlockSpec((ATT_BLOCK, ATT_BLOCK), lambda h, s: (0, 0)),
        ],
        out_specs=pl.BlockSpec((tqs, wcols), lambda h, s: (s, h)),
        out_shape=jax.ShapeDtypeStruct((rows, d_model), bf16),
        scratch_shapes=[pltpu.VMEM((ATT_HEADS, ATT_QSUB, dh, ATT_BLOCK), f32),
                        pltpu.VMEM((ATT_HEADS, ATT_QSUB, 1, ATT_BLOCK), f32)],
        compiler_params=_cparams(("arbitrary", "arbitrary")),
        name="sb_attention",
    )(qkv, qkv, qkv, qkv, qkv, u)


def _pack_rows(x, w_ref, z_ref):
    m, d = x.shape
    half = d // 2
    for c in range(half // LANES):
        z_ref[pl.ds(0, m, stride=2), :] = x[:, c * LANES:(c + 1) * LANES]
        z_ref[pl.ds(1, m, stride=2), :] = x[:, half + c * LANES:half + (c + 1) * LANES]
        words = pltpu.bitcast(z_ref[...].astype(bf16), jnp.uint32)
        w_ref[:, c, :, :] = words.reshape(m // SUBLANES, SUBLANES, LANES)


def _unpack_rows(w_ref, z_ref):
    m = w_ref.shape[0] * SUBLANES
    lo, hi = [], []
    for c in range(w_ref.shape[1]):
        words = w_ref[:, c, :, :].reshape(m, LANES)
        z_ref[...] = pltpu.bitcast(words, bf16).astype(f32)
        lo.append(z_ref[pl.ds(0, m, stride=2), :].astype(bf16))
        hi.append(z_ref[pl.ds(1, m, stride=2), :].astype(bf16))
    return jnp.concatenate(lo + hi, axis=1)


def _row_tiled(rows, width):
    return (rows // SUBLANES, width // LANES, SUBLANES, LANES)


def _row_of(ref, r):
    return ref.at[lax.shift_right_logical(r, LOG2_SUBLANES), :, pl.ds(r & (SUBLANES - 1), 1), :]


def _route_rows(h, g, w_ref, b, xp_ref, z_ref, earlier, base, n_experts):
    epg = n_experts // N_GROUPS
    xn = _rms_scale(h) * g
    _pack_rows(xn, xp_ref, z_ref)

    xh = xn.astype(bf16)
    xl = (xn - xh.astype(f32)).astype(bf16)
    hh_hl = jnp.dot(xh, w_ref[...], preferred_element_type=f32)
    logits = (hh_hl[:, :LANES] + hh_hl[:, LANES:]
              + jnp.dot(xl, w_ref[:, :LANES], preferred_element_type=f32) + b)
    lane = lax.broadcasted_iota(jnp.int32, logits.shape, 1)
    neg = jnp.float32(-jnp.inf)
    big = jnp.int32(LANES)

    gl = jnp.where(lane < N_GROUPS, logits, neg)
    gmax = jnp.max(gl, axis=-1, keepdims=True)
    g_sel = jnp.min(jnp.where(gl == gmax, lane, big), axis=-1, keepdims=True)
    p_top = 1.0 / jnp.sum(jnp.exp(gl - gmax), axis=-1, keepdims=True)

    lo_lane = N_GROUPS + g_sel * epg
    el = jnp.where((lane >= lo_lane) & (lane < lo_lane + epg), logits, neg)
    v1 = jnp.max(el, axis=-1, keepdims=True)
    l1 = jnp.min(jnp.where(el == v1, lane, big), axis=-1, keepdims=True)
    el2 = jnp.where(lane == l1, neg, el)
    v2 = jnp.max(el2, axis=-1, keepdims=True)
    l2 = jnp.min(jnp.where(el2 == v2, lane, big), axis=-1, keepdims=True)
    r = jnp.exp(v2 - v1)
    gate1 = p_top * (1.0 / (1.0 + r))
    gate2 = p_top * (r / (1.0 + r))

    onehot = ((lane == l1) | (lane == l2)).astype(f32)
    before = jnp.dot(earlier, onehot.astype(bf16), preferred_element_type=f32) + base
    rank1 = jnp.sum(jnp.where(lane == l1, before, 0.0), axis=-1, keepdims=True)
    rank2 = jnp.sum(jnp.where(lane == l2, before, 0.0), axis=-1, keepdims=True)
    base = base + jnp.sum(onehot, axis=0, keepdims=True)

    e1 = jnp.clip(l1 - N_GROUPS, 0, n_experts - 1)
    e2 = jnp.clip(l2 - N_GROUPS, 0, n_experts - 1)
    info = jnp.where(lane == 0, e1, 0)
    info = jnp.where(lane == 1, e2, info)
    info = jnp.where(lane == 2, rank1.astype(jnp.int32), info)
    info = jnp.where(lane == 3, rank2.astype(jnp.int32), info)
    info = jnp.where(lane == 4, pltpu.bitcast(jnp.broadcast_to(gate1, logits.shape), jnp.int32), info)
    info = jnp.where(lane == 5, pltpu.bitcast(jnp.broadcast_to(gate2, logits.shape), jnp.int32), info)
    return info, base


def _router_kernel(h_ref, g_ref, w_ref, b_ref, xp_ref, info_ref, infot_ref, cnt_ref, base_ref,
                   z_ref, *, n_experts):
    tm = h_ref.shape[0]

    @pl.when(pl.program_id(0) == 0)
    def _():
        base_ref[...] = jnp.zeros(base_ref.shape, f32)

    r_i = lax.broadcasted_iota(jnp.int32, (tm, tm), 0)
    c_i = lax.broadcasted_iota(jnp.int32, (tm, tm), 1)
    earlier = jnp.where(c_i < r_i, 1.0, 0.0).astype(bf16)
    info, base = _route_rows(h_ref[...], g_ref[...], w_ref, b_ref[...], xp_ref, z_ref, earlier,
                             base_ref[...], n_experts)
    info_ref[...] = info
    infot_ref[...] = info.T[0:INFO_ROWS, :]
    base_ref[...] = base
    cnt_ref[...] = base.astype(jnp.int32)


def _router(h, g, w_r, b_r, *, n_experts, tm):
    rows, d = h.shape
    kern = functools.partial(_router_kernel, n_experts=n_experts)
    return pl.pallas_call(
        kern,
        grid=(rows // tm,),
        in_specs=[
            pl.BlockSpec((tm, d), lambda i: (i, 0)),
            pl.BlockSpec((1, d), lambda i: (0, 0)),
            pl.BlockSpec((d, 2 * LANES), lambda i: (0, 0)),
            pl.BlockSpec((1, LANES), lambda i: (0, 0)),
        ],
        out_specs=[
            pl.BlockSpec(_row_tiled(tm, d // 2), lambda i: (i, 0, 0, 0)),
            pl.BlockSpec((tm, LANES), lambda i: (i, 0)),
            pl.BlockSpec((INFO_ROWS, tm), lambda i: (0, i)),
            pl.BlockSpec((1, LANES), lambda i: (0, 0)),
        ],
        out_shape=[
            jax.ShapeDtypeStruct(_row_tiled(rows, d // 2), jnp.uint32),
            jax.ShapeDtypeStruct((rows, LANES), jnp.int32),
            jax.ShapeDtypeStruct((INFO_ROWS, rows), jnp.int32),
            jax.ShapeDtypeStruct((1, LANES), jnp.int32),
        ],
        scratch_shapes=[pltpu.VMEM((1, LANES), f32), pltpu.VMEM((2 * tm, LANES), f32)],
        compiler_params=_cparams(("arbitrary",)),
        name="moe_router",
    )(h, g, w_r, b_r)


def _matmul_residual_kernel(a_ref, w_ref, r_ref, o_ref):
    o_ref[...] = r_ref[...] + jnp.dot(a_ref[...], w_ref[...], preferred_element_type=f32)


def _matmul_residual_split_kernel(a_ref, at_ref, w_ref, r_ref, rt_ref, o_ref, *, n_main):
    @pl.when(pl.program_id(0) < n_main)
    def _():
        o_ref[...] = r_ref[...] + jnp.dot(a_ref[...], w_ref[...], preferred_element_type=f32)

    @pl.when(pl.program_id(0) >= n_main)
    def _():
        o_ref[...] = rt_ref[...] + jnp.dot(at_ref[...], w_ref[...], preferred_element_type=f32)


def _matmul_residual(a, w, r, *, tm):
    rows, k = a.shape
    n = w.shape[1]
    return pl.pallas_call(
        _matmul_residual_kernel,
        grid=(rows // tm,),
        in_specs=[pl.BlockSpec((tm, k), lambda i: (i, 0)),
                  pl.BlockSpec((k, n), lambda i: (0, 0)),
                  pl.BlockSpec((tm, n), lambda i: (i, 0))],
        out_specs=pl.BlockSpec((tm, n), lambda i: (i, 0)),
        out_shape=jax.ShapeDtypeStruct((rows, n), f32),
        compiler_params=_cparams(("arbitrary",)),
        name="out_proj_residual",
    )(a, w, r)


def _matmul_residual_split(a, a_tail, w, r, r_tail, *, tm):
    rows, k = a.shape
    n = w.shape[1]
    n_main = rows // tm
    assert rows % tm == 0 and a_tail.shape[0] == tm and r.shape[0] == rows and r_tail.shape[0] == tm
    kern = functools.partial(_matmul_residual_split_kernel, n_main=n_main)

    def main_blk(i):
        return (jnp.minimum(i, n_main - 1), 0)

    return pl.pallas_call(
        kern,
        grid=(n_main + 1,),
        in_specs=[pl.BlockSpec((tm, k), main_blk),
                  pl.BlockSpec((tm, k), lambda i: (0, 0)),
                  pl.BlockSpec((k, n), lambda i: (0, 0)),
                  pl.BlockSpec((tm, n), main_blk),
                  pl.BlockSpec((tm, n), lambda i: (0, 0))],
        out_specs=pl.BlockSpec((tm, n), lambda i: (i, 0)),
        out_shape=jax.ShapeDtypeStruct((rows + tm, n), f32),
        compiler_params=_cparams(("arbitrary",)),
        name="out_proj_residual",
    )(a, a_tail, w, r, r_tail)


def _pad_fill_copies(fs_ref, pad_ref, zbuf, xs_ref, sem, e):
    fs = fs_ref[e]
    pad = pad_ref[e]
    head = pad & (SUBLANES - 1)
    out = []
    for k in range(SUBLANES - 1):
        out.append((k < head,
                    pltpu.make_async_copy(zbuf.at[0, :, pl.ds(0, 1), :], _row_of(xs_ref, fs + k), sem)))
    cur = lax.shift_right_logical(fs + head, LOG2_SUBLANES)
    for b in reversed(range(SUBLANES.bit_length() - 1, MOE_TILE.bit_length() - 1)):
        size = 1 << b
        groups = size // SUBLANES
        out.append(((pad & size) != 0,
                    pltpu.make_async_copy(zbuf.at[pl.ds(0, groups)], xs_ref.at[pl.ds(cur, groups)], sem)))
        cur = cur + lax.shift_right_logical(pad & size, LOG2_SUBLANES)
    return out


def _unused_tile_copies(zbuf, xs_ref, sem, t):
    groups = zbuf.shape[0]
    per_tile = MOE_TILE // SUBLANES
    return [pltpu.make_async_copy(zbuf, xs_ref.at[pl.ds(t * per_tile + k * groups, groups)], sem)
            for k in range(per_tile // groups)]


def _dispatch_kernel(p0_ref, p1_ref, fs_ref, pad_ref, nu_ref, x_ref, xs_ref, zbuf, sem, zsem, *,
                     n_steps, n_experts, max_tiles):
    i = pl.program_id(0)
    tm = x_ref.shape[0] * SUBLANES
    base = i * tm

    @pl.when(i == 0)
    def _():
        zbuf[...] = jnp.zeros(zbuf.shape, zbuf.dtype)

        def fill(e, c):
            for needed, copy in _pad_fill_copies(fs_ref, pad_ref, zbuf, xs_ref, zsem, e):
                @pl.when(needed)
                def _():
                    copy.start()
            return c

        lax.fori_loop(0, n_experts, fill, 0)

        def fill_unused(t, c):
            for copy in _unused_tile_copies(zbuf, xs_ref, zsem, t):
                copy.start()
            return c

        lax.fori_loop(nu_ref[0], max_tiles, fill_unused, 0)

    def issue(g, c):
        for k in range(SUBLANES):
            src = x_ref.at[g, :, pl.ds(k, 1), :]
            r = base + g * SUBLANES + k
            pltpu.make_async_copy(src, _row_of(xs_ref, p0_ref[r]), sem).start(priority=0)
            pltpu.make_async_copy(src, _row_of(xs_ref, p1_ref[r]), sem).start(priority=1)
        return c

    lax.fori_loop(0, tm // SUBLANES, issue, 0)
    for _ in range(TOP_K):
        pltpu.make_async_copy(x_ref, xs_ref.at[pl.ds(0, tm // SUBLANES)], sem).wait()

    @pl.when(i == n_steps - 1)
    def _():
        def drain(e, c):
            for needed, copy in _pad_fill_copies(fs_ref, pad_ref, zbuf, xs_ref, zsem, e):
                @pl.when(needed)
                def _():
                    copy.wait()
            return c

        lax.fori_loop(0, n_experts, drain, 0)

        def drain_unused(t, c):
            for copy in _unused_tile_copies(zbuf, xs_ref, zsem, t):
                copy.wait()
            return c

        lax.fori_loop(nu_ref[0], max_tiles, drain_unused, 0)


def _dispatch(pos0, pos1, fill_start, pad, n_used, xp, *, sorted_rows, tm):
    width = xp.shape[1] * LANES
    n_steps = (xp.shape[0] * SUBLANES) // tm
    kern = functools.partial(_dispatch_kernel, n_steps=n_steps, n_experts=pad.shape[0],
                             max_tiles=sorted_rows // MOE_TILE)
    return pl.pallas_call(
        kern,
        grid_spec=pltpu.PrefetchScalarGridSpec(
            num_scalar_prefetch=5,
            grid=(n_steps,),
            in_specs=[pl.BlockSpec(_row_tiled(tm, width), lambda i, *_: (i, 0, 0, 0))],
            out_specs=pl.BlockSpec(memory_space=pl.ANY),
            scratch_shapes=[pltpu.VMEM(_row_tiled(MOE_TILE // 2, width), jnp.uint32),
                            pltpu.SemaphoreType.DMA(()), pltpu.SemaphoreType.DMA(())],
        ),
        out_shape=jax.ShapeDtypeStruct(_row_tiled(sorted_rows, width), jnp.uint32),
        compiler_params=pltpu.CompilerParams(dimension_semantics=("arbitrary",),
                                             vmem_limit_bytes=VMEM_LIMIT, has_side_effects=True),
        name="moe_dispatch",
    )(pos0, pos1, fill_start, pad, n_used, xp)


def _experts_kernel(te_ref, tf_ref, nx_ref, nu_ref, xs_ref, wg_hbm, wu_hbm, wd_hbm, y_ref,
                    wg_f, wu_f, wd_f, wg_b, wu_b, wd_b, z_ref, slot_ref, sems, *, layer):
    t = pl.program_id(0)

    def weight_copies(e, slot):
        return [pltpu.make_async_copy(hbm.at[layer, e], buf.at[slot], sems.at[slot, k])
                for k, (hbm, buf) in enumerate(((wg_hbm, wg_f), (wu_hbm, wu_f), (wd_hbm, wd_f)))]

    @pl.when(t == 0)
    def _():
        slot_ref[0] = 0
        for c in weight_copies(te_ref[0], 0):
            c.start()

    @pl.when(tf_ref[t] == 1)
    def _():
        slot = slot_ref[0]
        for c in weight_copies(te_ref[t], slot):
            c.wait()

        @pl.when(nx_ref[t] >= 0)
        def _():
            for c in weight_copies(nx_ref[t], 1 - slot):
                c.start()

        wg_b[...] = wg_f[slot].astype(bf16)
        wu_b[...] = wu_f[slot].astype(bf16)
        wd_b[...] = wd_f[slot].astype(bf16)
        slot_ref[0] = 1 - slot

    @pl.when(t < nu_ref[0])
    def _():
        x = _unpack_rows(xs_ref, z_ref)
        gate = jnp.dot(x, wg_b[...], preferred_element_type=f32)
        up = jnp.dot(x, wu_b[...], preferred_element_type=f32)
        hidden = (gate * (1.0 / (1.0 + jnp.exp(-gate))) * up).astype(bf16)
        _pack_rows(jnp.dot(hidden, wd_b[...], preferred_element_type=f32), y_ref, z_ref)

    @pl.when(t >= nu_ref[0])
    def _():
        y_ref[...] = jnp.zeros(y_ref.shape, y_ref.dtype)


def _experts(tile_expert, tile_first, next_expert, n_used, xs, w_gate, w_up, w_down, *, layer,
             max_tiles):
    tm = MOE_TILE
    _, _, d, f = w_gate.shape
    kern = functools.partial(_experts_kernel, layer=layer)
    hbm = pl.BlockSpec(memory_space=pl.ANY)
    return pl.pallas_call(
        kern,
        grid_spec=pltpu.PrefetchScalarGridSpec(
            num_scalar_prefetch=4,
            grid=(max_tiles,),
            in_specs=[pl.BlockSpec(_row_tiled(tm, d // 2),
                                   lambda t, te, tf, nx, nu: (jnp.clip(nu[0] - 1, 0, t), 0, 0, 0)),
                      hbm, hbm, hbm],
            out_specs=pl.BlockSpec(_row_tiled(tm, d // 2), lambda t, *_: (t, 0, 0, 0)),
            scratch_shapes=[pltpu.VMEM((2, d, f), f32), pltpu.VMEM((2, d, f), f32),
                            pltpu.VMEM((2, f, d), f32),
                            pltpu.VMEM((d, f), bf16), pltpu.VMEM((d, f), bf16),
                            pltpu.VMEM((f, d), bf16), pltpu.VMEM((2 * tm, LANES), f32),
                            pltpu.SMEM((1,), jnp.int32), pltpu.SemaphoreType.DMA((2, 3))],
        ),
        out_shape=jax.ShapeDtypeStruct(_row_tiled(max_tiles * tm, d // 2), jnp.uint32),
        compiler_params=_cparams(("arbitrary",)),
        name="moe_experts",
    )(tile_expert, tile_first, next_expert, n_used, xs, w_gate, w_up, w_down)


def _combine_kernel(p0_ref, p1_ref, h_ref, info_ref, gf_ref, ys_ref, o_ref, buf, z_ref, sems, *,
                    n_steps, final_norm):
    i = pl.program_id(0)
    tm = h_ref.shape[0]

    def issue(step, slot):
        base = step * tm

        def body(g, c):
            for k in range(SUBLANES):
                r = base + g * SUBLANES + k
                pltpu.make_async_copy(_row_of(ys_ref, p0_ref[r]),
                                      buf.at[slot, 0, g, :, pl.ds(k, 1), :],
                                      sems.at[slot]).start(priority=0)
                pltpu.make_async_copy(_row_of(ys_ref, p1_ref[r]),
                                      buf.at[slot, 1, g, :, pl.ds(k, 1), :],
                                      sems.at[slot]).start(priority=1)
            return c

        lax.fori_loop(0, tm // SUBLANES, body, 0)

    @pl.when(i == 0)
    def _():
        issue(0, 0)

    @pl.when(i + 1 < n_steps)
    def _():
        issue(i + 1, lax.rem(i + 1, 2))

    slot = lax.rem(i, 2)
    for k in range(TOP_K):
        pltpu.make_async_copy(ys_ref.at[pl.ds(0, tm // SUBLANES)], buf.at[slot, k],
                              sems.at[slot]).wait()

    info = info_ref[...]
    lane = lax.broadcasted_iota(jnp.int32, info.shape, 1)
    gates = pltpu.bitcast(info, f32)
    g1 = jnp.sum(jnp.where(lane == 4, gates, 0.0), axis=-1, keepdims=True)
    g2 = jnp.sum(jnp.where(lane == 5, gates, 0.0), axis=-1, keepdims=True)
    lo, hi = [], []
    for c in range(buf.shape[3]):
        for k in range(TOP_K):
            words = buf[slot, k, :, c, :, :].reshape(tm, LANES)
            z_ref[k] = pltpu.bitcast(words, bf16).astype(f32)
        lo.append(g1 * z_ref[0, pl.ds(0, tm, stride=2), :] + g2 * z_ref[1, pl.ds(0, tm, stride=2), :])
        hi.append(g1 * z_ref[0, pl.ds(1, tm, stride=2), :] + g2 * z_ref[1, pl.ds(1, tm, stride=2), :])
    out = h_ref[...] + jnp.concatenate(lo + hi, axis=1)
    if final_norm:
        out = _rms_scale(out) * gf_ref[...]
    o_ref[...] = out


def _combine(pos0, pos1, h, info, gf, ys, *, out_rows, tm, final_norm):
    d = h.shape[1]
    n_steps = out_rows // tm
    kern = functools.partial(_combine_kernel, n_steps=n_steps, final_norm=final_norm)
    return pl.pallas_call(
        kern,
        grid_spec=pltpu.PrefetchScalarGridSpec(
            num_scalar_prefetch=2,
            grid=(n_steps,),
            in_specs=[
                pl.BlockSpec((tm, d), lambda i, p0, p1: (i, 0)),
                pl.BlockSpec((tm, LANES), lambda i, p0, p1: (i, 0)),
                pl.BlockSpec((1, d), lambda i, p0, p1: (0, 0)),
                pl.BlockSpec(memory_space=pl.ANY),
            ],
            out_specs=pl.BlockSpec((tm, d), lambda i, p0, p1: (i, 0)),
            scratch_shapes=[pltpu.VMEM((2, TOP_K) + _row_tiled(tm, d // 2), jnp.uint32),
                            pltpu.VMEM((TOP_K, 2 * tm, LANES), f32),
                            pltpu.SemaphoreType.DMA((2,))],
        ),
        out_shape=jax.ShapeDtypeStruct((out_rows, d), f32),
        compiler_params=_cparams(("arbitrary",)),
        name="moe_combine",
    )(pos0, pos1, h, info, gf, ys)


def _table_lookup(table, idx):
    k = lax.broadcasted_iota(jnp.int32, (table.shape[0], idx.shape[0]), 0)
    return jnp.sum(jnp.where(k == idx[None, :], table[:, None], 0), axis=0)


def _router_params(wg_r, bg_r, we_r, be_r):
    d, n_experts = we_r.shape
    n_fill = LANES - N_GROUPS - n_experts
    w_r = jnp.concatenate([wg_r, we_r, jnp.zeros((d, n_fill), f32)], axis=1)
    b_r = jnp.concatenate([bg_r, be_r, jnp.zeros((n_fill,), f32)]).reshape(1, LANES)
    w_hi = w_r.astype(bf16)
    return jnp.concatenate([w_hi, (w_r - w_hi.astype(f32)).astype(bf16)], axis=1), b_r


def _mixer_out_and_moe(a, a_tail, w_out, r, r_tail, g_ffn, wg_r, bg_r, we_r, be_r, w_gate, w_up,
                       w_down, gf, *, layer, out_rows, final_norm):
    d = w_out.shape[1]
    n_experts = we_r.shape[1]
    w_hl, b_r = _router_params(wg_r, bg_r, we_r, be_r)
    if a_tail is None:
        h = _matmul_residual(a, w_out, r, tm=TAIL_ROWS)
    else:
        h = _matmul_residual_split(a, a_tail, w_out, r, r_tail, tm=TAIL_ROWS)
    rows = h.shape[0]
    xp, info, info_t, cnt = _router(h, g_ffn.reshape(1, d), w_hl, b_r, n_experts=n_experts,
                                    tm=TAIL_ROWS)

    tm = MOE_TILE
    max_tiles = (TOP_K * rows) // tm + n_experts
    i32 = jnp.int32
    cnt = cnt[0, N_GROUPS:N_GROUPS + n_experts]
    n_tiles = (cnt + tm - 1) // tm
    tile_end = jnp.cumsum(n_tiles)
    tile_start = tile_end - n_tiles
    n_used = tile_end[-1]
    t_ids = jnp.arange(max_tiles, dtype=i32)[None, :]
    e_ids = jnp.arange(n_experts, dtype=i32)[:, None]
    first_t, end_t = tile_start[:, None], tile_end[:, None]
    member = (first_t <= t_ids) & (t_ids < end_t)
    te = jnp.sum(jnp.where(member, e_ids, 0), axis=0)
    tile_first = jnp.any(member & (first_t == t_ids), axis=0).astype(i32)
    nxt_tile = jnp.sum(jnp.where(member, end_t, 0), axis=0)[None, :]
    is_next = (first_t == nxt_tile) & (n_tiles[:, None] > 0)
    next_expert = jnp.where(jnp.any(is_next, axis=0), jnp.sum(jnp.where(is_next, e_ids, 0), axis=0), -1)

    sorted_rows = max_tiles * tm
    row_off = (tile_start * tm).astype(i32)
    pos0 = jnp.clip(_table_lookup(row_off, info_t[0]) + info_t[2], 0, sorted_rows - 1)
    pos1 = jnp.clip(_table_lookup(row_off, info_t[1]) + info_t[3], 0, sorted_rows - 1)

    n_used = n_used.reshape(1).astype(i32)
    xs = _dispatch(pos0, pos1, (row_off + cnt).astype(i32), (n_tiles * tm - cnt).astype(i32), n_used,
                   xp, sorted_rows=sorted_rows, tm=DISPATCH_ROWS)
    ys = _experts(te.astype(i32), tile_first, next_expert.astype(i32), n_used,
                  xs, w_gate, w_up, w_down, layer=layer, max_tiles=max_tiles)
    return _combine(pos0, pos1, h, info, gf.reshape(1, d), ys, out_rows=out_rows, tm=COMBINE_ROWS,
                    final_norm=final_norm)


def _strict_upper_ones(n):
    r = lax.broadcasted_iota(jnp.int32, (n, n), 0)
    c = lax.broadcasted_iota(jnp.int32, (n, n), 1)
    return (c > r).astype(bf16)


def kernel(x, meta_tokens, norm_mix_g, norm_ffn_g, conv_w_in, conv_w, conv_w_out, attn_w_qkv, attn_w_out, router_group_w, router_group_b, router_expert_w, router_expert_b, moe_w_gate, moe_w_up, moe_w_down, final_norm_g):
    batch, seq_len, d = x.shape
    n_meta = meta_tokens.shape[0]
    n_pad = META_BLOCK - n_meta
    n_tok = batch * seq_len
    assert seq_len % TAIL_ROWS == 0 and d % (2 * N_HEADS) == 0

    x2d = x.reshape(n_tok, d)
    tail = jnp.concatenate([jnp.zeros((n_pad, d), f32), meta_tokens.astype(f32),
                            jnp.zeros((TAIL_ROWS - META_BLOCK, d), f32)], axis=0)

    tc = _largest_divisor(d, CONV_COL_TILES)
    g_mix0 = norm_mix_g[0].reshape(1, d)
    w_in = conv_w_in[0].astype(bf16)
    gated_tail, meta_v = _conv_in(tail, g_mix0, w_in, conv_w[0], jnp.zeros((SUBLANES, d), f32),
                                  seq_len=TAIL_ROWS, tm=TAIL_ROWS, tc=tc)
    gated, _ = _conv_in(x2d, g_mix0, w_in, conv_w[0], meta_v, seq_len=seq_len,
                        tm=_largest_divisor(seq_len, CONV_ROW_TILES), tc=tc)
    h = _mixer_out_and_moe(gated, gated_tail, conv_w_out[0].astype(bf16), x2d, tail, norm_ffn_g[0],
                           router_group_w[0], router_group_b[0], router_expert_w[0],
                           router_expert_b[0], moe_w_gate, moe_w_up, moe_w_down, final_norm_g,
                           layer=0, out_rows=n_tok + TAIL_ROWS, final_norm=False)

    qkv = _norm_matmul(h, norm_mix_g[1].reshape(1, d), attn_w_qkv[0].astype(bf16),
                       tm=_largest_divisor(h.shape[0], QKV_ROW_TILES),
                       tn=_largest_divisor(d, QKV_COL_TILES),
                       scaled_cols=d, scale=(d // N_HEADS) ** -0.5)
    o = _attention(qkv, _strict_upper_ones(ATT_BLOCK), batch=batch, seq_len=seq_len, d_model=d,
                   n_pad=n_pad)
    out = _mixer_out_and_moe(o, None, attn_w_out[0].astype(bf16), h, None, norm_ffn_g[1],
                             router_group_w[1], router_group_b[1], router_expert_w[1],
                             router_expert_b[1], moe_w_gate, moe_w_up, moe_w_down, final_norm_g,
                             layer=1, out_rows=n_tok, final_norm=True)
    return out.reshape(batch, seq_len, d)
```

```python
import functools

import jax
import jax.numpy as jnp
from jax import lax
from jax.experimental import pallas as pl
from jax.experimental.pallas import tpu as pltpu

N_HEADS = 16
N_GROUPS = 4
TOP_K = 2
CONV_WIDTH = 3
EPS = 1e-6

LANES = 128
SUBLANES = 8
LOG2_SUBLANES = SUBLANES.bit_length() - 1
META_BLOCK = 128
TAIL_ROWS = 512
MOE_TILE = 256
ATT_BLOCK = 256
ATT_HEADS = 8
ATT_QSUB = 2
INFO_ROWS = 8
CONV_ROW_TILES = (1024, 512)
CONV_COL_TILES = (512, 256)
QKV_ROW_TILES = (1536, 512)
QKV_COL_TILES = (1024, 512, 256, 128)
DISPATCH_ROWS = 256
COMBINE_ROWS = 128
SKIP_BELOW = -104.0
LOG2E = 1.4426950408889634
VMEM_LIMIT = 56 * 1024 * 1024

f32 = jnp.float32
bf16 = jnp.bfloat16


def _cparams(sem):
    return pltpu.CompilerParams(dimension_semantics=sem, vmem_limit_bytes=VMEM_LIMIT)


def _largest_divisor(n, candidates):
    return next(c for c in candidates if n % c == 0)


def _rms_scale(x):
    return x * lax.rsqrt(jnp.mean(x * x, axis=-1, keepdims=True) + EPS)


def _norm_matmul_kernel(h_ref, g_ref, w_ref, o_ref, xn_ref, *, n_scaled, scale):
    j = pl.program_id(1)

    @pl.when(j == 0)
    def _():
        xn_ref[...] = (_rms_scale(h_ref[...]) * g_ref[...]).astype(bf16)

    mult = jnp.where(j < n_scaled, scale, 1.0)
    o_ref[...] = (jnp.dot(xn_ref[...], w_ref[...], preferred_element_type=f32) * mult).astype(o_ref.dtype)


def _norm_matmul(h, g, w, *, tm, tn, scaled_cols, scale):
    rows, d = h.shape
    n = w.shape[1]
    assert scaled_cols % tn == 0
    kern = functools.partial(_norm_matmul_kernel, n_scaled=scaled_cols // tn, scale=scale)
    return pl.pallas_call(
        kern,
        grid=(rows // tm, n // tn),
        in_specs=[
            pl.BlockSpec((tm, d), lambda i, j: (i, 0)),
            pl.BlockSpec((1, d), lambda i, j: (0, 0)),
            pl.BlockSpec((d, tn), lambda i, j: (0, j)),
        ],
        out_specs=pl.BlockSpec((tm, tn), lambda i, j: (i, j)),
        out_shape=jax.ShapeDtypeStruct((rows, n), bf16),
        scratch_shapes=[pltpu.VMEM((tm, d), bf16)],
        compiler_params=_cparams(("arbitrary", "arbitrary")),
        name="norm_qkv",
    )(h, g, w)


def _conv_in_kernel(x_ref, g_ref, wb_ref, wc_ref, wh_ref, cw_ref, lead_ref, o_ref, meta_v_ref,
                    xn_ref, carry_ref, *, tiles_per_seq):
    i = pl.program_id(0)
    j = pl.program_id(1)
    tm, tc = o_ref.shape

    @pl.when(j == 0)
    def _():
        xn_ref[...] = (_rms_scale(x_ref[...]) * g_ref[...]).astype(bf16)

    xn = xn_ref[...]
    seq_start = lax.rem(i, tiles_per_seq) == 0
    row = lax.broadcasted_iota(jnp.int32, (tm, tc // 2), 0)
    halves = (slice(0, tc // 2), slice(tc // 2, tc))
    projected = [tuple(jnp.dot(xn, w[:, cols], preferred_element_type=f32)
                       for w in (wb_ref, wc_ref, wh_ref)) for cols in halves]
    for cols, (b_gate, c_gate, hh) in zip(halves, projected):
        v = c_gate * hh
        prev = jnp.where(seq_start, lead_ref[:, cols], carry_ref[j, :, cols])
        p1 = prev[SUBLANES - 1:SUBLANES, :]
        p2 = prev[SUBLANES - 2:SUBLANES - 1, :]
        s1 = jnp.where(row == 0, p1, pltpu.roll(v, 1, 0))
        s2 = jnp.where(row == 0, p2, jnp.where(row == 1, p1, pltpu.roll(v, 2, 0)))
        y = cw_ref[0:1, cols] * s2 + cw_ref[1:2, cols] * s1 + cw_ref[2:3, cols] * v
        o_ref[:, cols] = (b_gate * y).astype(o_ref.dtype)
        carry_ref[j, :, cols] = v[tm - SUBLANES:, :]
        meta_v_ref[:, cols] = v[META_BLOCK - SUBLANES:META_BLOCK, :]


def _conv_in(x2d, g, w_in, conv_w, lead, *, seq_len, tm, tc):
    rows, d = x2d.shape
    n_j = d // tc
    assert seq_len % tm == 0 and rows % tm == 0
    kern = functools.partial(_conv_in_kernel, tiles_per_seq=seq_len // tm)
    return pl.pallas_call(
        kern,
        grid=(rows // tm, n_j),
        in_specs=[
            pl.BlockSpec((tm, d), lambda i, j: (i, 0)),
            pl.BlockSpec((1, d), lambda i, j: (0, 0)),
            pl.BlockSpec((d, tc), lambda i, j: (0, j)),
            pl.BlockSpec((d, tc), lambda i, j: (0, n_j + j)),
            pl.BlockSpec((d, tc), lambda i, j: (0, 2 * n_j + j)),
            pl.BlockSpec((CONV_WIDTH, tc), lambda i, j: (0, j)),
            pl.BlockSpec((SUBLANES, tc), lambda i, j: (0, j)),
        ],
        out_specs=[pl.BlockSpec((tm, tc), lambda i, j: (i, j)),
                   pl.BlockSpec((SUBLANES, tc), lambda i, j: (i, j))],
        out_shape=[jax.ShapeDtypeStruct((rows, d), bf16),
                   jax.ShapeDtypeStruct((SUBLANES * (rows // tm), d), f32)],
        scratch_shapes=[pltpu.VMEM((tm, d), bf16), pltpu.VMEM((n_j, SUBLANES, tc), f32)],
        compiler_params=_cparams(("arbitrary", "arbitrary")),
        name="conv_in",
    )(x2d, g, w_in, w_in, w_in, conv_w, lead)


def _sb_scores(q, kb):
    return lax.dot_general(kb, q, (((1,), (1,)), ((), ())), preferred_element_type=f32)


def _sb_logs(z, mask):
    softplus_neg = jnp.log(1.0 + jnp.exp2(jnp.abs(z) * (-LOG2E)))
    log_sig = jnp.minimum(z, 0.0) - softplus_neg
    log_1m = log_sig - z
    if mask is not None:
        log_1m = jnp.where(mask, log_1m, 0.0)
    return log_sig, jnp.sum(log_1m, axis=0, keepdims=True), log_1m.astype(bf16)


def _sb_suffix_sums(u, log_1m):
    return jnp.dot(u, log_1m, preferred_element_type=f32)


def _sb_weights(log_sig, log_1m_sum, rest, carry, mask):
    a = jnp.exp2((log_sig + rest + carry) * LOG2E)
    if mask is not None:
        a = jnp.where(mask, a, 0.0)
    return a.astype(bf16), carry + log_1m_sum


def _sb_values(vb, a):
    return lax.dot_general(vb, a, (((0,), (0,)), ((), ())), preferred_element_type=f32)


def _sb_block(q, kb, vb, u, carry, mask):
    log_sig, total, log_1m = _sb_logs(_sb_scores(q, kb), mask)
    a, carry = _sb_weights(log_sig, total, _sb_suffix_sums(u, log_1m), carry, mask)
    return _sb_values(vb, a), carry


def _attn_kernel(q_ref, k_ref, v_ref, km_ref, vm_ref, u_ref, o_ref, acc_ref, carry_ref, *,
                 n_pad, n_main_steps):
    step = pl.program_id(1)
    steps_per_seq = k_ref.shape[0] // (ATT_QSUB * ATT_BLOCK)
    qstep = lax.rem(step, steps_per_seq)
    tb = ATT_BLOCK
    dh = q_ref.shape[1] // ATT_HEADS
    u = u_ref[...]
    s_loc = lax.broadcasted_iota(jnp.int32, (tb, tb), 0)
    t_loc = lax.broadcasted_iota(jnp.int32, (tb, tb), 1)
    causal = s_loc < t_loc
    meta_keys = (s_loc >= n_pad) & (s_loc < META_BLOCK)

    @pl.when(step < n_main_steps)
    def _():
        chains = [(hh, qs) for hh in range(ATT_HEADS) for qs in range(ATT_QSUB)]
        first = qstep == 0

        def keys(ref, mref, hh, qs, prev):
            cols = slice(hh * dh, (hh + 1) * dh)
            i = qstep * ATT_QSUB + qs
            blk = jnp.maximum(i - 1, 0) if prev else i
            kv = ref[pl.ds(pl.multiple_of(blk * tb, tb), tb), cols]
            if prev and qs == 0:
                kv = jnp.where(first, mref[:, cols], kv)
            return kv

        items = [(hh, qs, False) for hh, qs in chains] + [(hh, qs, True) for hh, qs in chains]
        n_items = len(items)
        masks = [(meta_keys | jnp.logical_not(first)) if (prev and qs == 0)
                 else (None if prev else causal) for hh, qs, prev in items]
        z, logs, rest, wts, contrib, carry_out = {}, {}, {}, {}, {}, {}
        for t in range(n_items + 4):
            if 0 <= t - 4 < n_items:
                n = t - 4
                hh, qs, prev = items[n]
                contrib[n] = _sb_values(keys(v_ref, vm_ref, hh, qs, prev), wts.pop(n))
            if 0 <= t - 3 < n_items:
                n = t - 3
                hh, qs, prev = items[n]
                carry_in = carry_out[n - len(chains)] if prev else jnp.zeros((1, tb), f32)
                log_sig, row0, _ = logs.pop(n)
                wts[n], carry_out[n] = _sb_weights(log_sig, row0, rest.pop(n), carry_in, masks[n])
            if 0 <= t - 2 < n_items:
                rest[t - 2] = _sb_suffix_sums(u, logs[t - 2][2])
            if 0 <= t - 1 < n_items:
                logs[t - 1] = _sb_logs(z.pop(t - 1), masks[t - 1])
            if t < n_items:
                hh, qs, prev = items[t]
                q = q_ref[qs * tb:(qs + 1) * tb, hh * dh:(hh + 1) * dh]
                z[t] = _sb_scores(q, keys(k_ref, km_ref, hh, qs, prev))
        for c, (hh, qs) in enumerate(chains):
            acc_ref[hh, qs] = contrib[c] + contrib[c + len(chains)]
            carry_ref[hh, qs] = carry_out[c + len(chains)]

        top_all = jnp.max(functools.reduce(
            jnp.maximum, [carry_out[c + len(chains)] for c in range(len(chains))]))

        @pl.when(top_all > SKIP_BELOW)
        def _():
            for hh, qs in chains:
                cols = slice(hh * dh, (hh + 1) * dh)
                i = qstep * ATT_QSUB + qs

                def cond(state):
                    j, top = state
                    return (j >= 0) & (top > SKIP_BELOW)

                def body(state, hh=hh, qs=qs, cols=cols):
                    j, _ = state
                    st = pl.multiple_of(j * tb, tb)
                    q = q_ref[qs * tb:(qs + 1) * tb, cols]
                    contrib, carry = _sb_block(q, k_ref[pl.ds(st, tb), cols],
                                               v_ref[pl.ds(st, tb), cols], u, carry_ref[hh, qs],
                                               None)
                    acc_ref[hh, qs] += contrib
                    carry_ref[hh, qs] = carry
                    return j - 1, jnp.max(carry)

                _, top = lax.while_loop(cond, body, (i - 2, jnp.max(carry_ref[hh, qs])))

                @pl.when((top > SKIP_BELOW) & (i >= 1))
                def _(hh=hh, qs=qs, cols=cols):
                    q = q_ref[qs * tb:(qs + 1) * tb, cols]
                    contrib, _ = _sb_block(q, km_ref[:, cols], vm_ref[:, cols], u,
                                           carry_ref[hh, qs], meta_keys)
                    acc_ref[hh, qs] += contrib

        for hh, qs in chains:
            o_ref[qs * tb:(qs + 1) * tb, hh * dh:(hh + 1) * dh] = acc_ref[hh, qs].T.astype(o_ref.dtype)

    @pl.when(step >= n_main_steps)
    def _():
        o_ref[...] = jnp.zeros(o_ref.shape, o_ref.dtype)
        for hh in range(ATT_HEADS):
            cols = slice(hh * dh, (hh + 1) * dh)
            contrib, _ = _sb_block(q_ref[0:tb, cols], km_ref[:, cols], vm_ref[:, cols], u,
                                   jnp.zeros((1, tb), f32), causal & meta_keys)
            o_ref[0:tb, cols] = contrib.T.astype(o_ref.dtype)


def _attention(qkv, u, *, batch, seq_len, d_model, n_pad):
    rows = qkv.shape[0]
    dh = d_model // N_HEADS
    tqs = ATT_QSUB * ATT_BLOCK
    assert tqs == TAIL_ROWS and seq_len % tqs == 0 and N_HEADS % ATT_HEADS == 0
    steps_per_seq = seq_len // tqs
    n_main = batch * steps_per_seq
    n_hp = N_HEADS // ATT_HEADS
    wcols = ATT_HEADS * dh
    meta_blk = (batch * seq_len) // ATT_BLOCK
    kern = functools.partial(_attn_kernel, n_pad=n_pad, n_main_steps=n_main)

    def seq_of(s):
        return jnp.minimum(s // steps_per_seq, batch - 1)

    return pl.pallas_call(
        kern,
        grid=(n_hp, n_main + 1),
        in_specs=[
            pl.BlockSpec((tqs, wcols), lambda h, s: (s, h)),
            pl.BlockSpec((seq_len, wcols), lambda h, s: (seq_of(s), n_hp + h)),
            pl.BlockSpec((seq_len, wcols), lambda h, s: (seq_of(s), 2 * n_hp + h)),
            pl.BlockSpec((ATT_BLOCK, wcols), lambda h, s: (meta_blk, n_hp + h)),
            pl.BlockSpec((ATT_BLOCK, wcols), lambda h, s: (meta_blk, 2 * n_hp + h)),
            pl.BlockSpec((ATT_BLOCK, ATT_BLOCK), lambda h, s: (0, 0)),
        ],
        out_specs=pl.BlockSpec((tqs, wcols), lambda h, s: (s, h)),
        out_shape=jax.ShapeDtypeStruct((rows, d_model), bf16),
        scratch_shapes=[pltpu.VMEM((ATT_HEADS, ATT_QSUB, dh, ATT_BLOCK), f32),
                        pltpu.VMEM((ATT_HEADS, ATT_QSUB, 1, ATT_BLOCK), f32)],
        compiler_params=_cparams(("arbitrary", "arbitrary")),
        name="sb_attention",
    )(qkv, qkv, qkv, qkv, qkv, u)


def _pack_rows(x, w_ref, z_ref):
    m, d = x.shape
    half = d // 2
    for c in range(half // LANES):
        z_ref[pl.ds(0, m, stride=2), :] = x[:, c * LANES:(c + 1) * LANES]
        z_ref[pl.ds(1, m, stride=2), :] = x[:, half + c * LANES:half + (c + 1) * LANES]
        words = pltpu.bitcast(z_ref[...].astype(bf16), jnp.uint32)
        w_ref[:, c, :, :] = words.reshape(m // SUBLANES, SUBLANES, LANES)


def _unpack_rows(w_ref, z_ref):
    m = w_ref.shape[0] * SUBLANES
    lo, hi = [], []
    for c in range(w_ref.shape[1]):
        words = w_ref[:, c, :, :].reshape(m, LANES)
        z_ref[...] = pltpu.bitcast(words, bf16).astype(f32)
        lo.append(z_ref[pl.ds(0, m, stride=2), :].astype(bf16))
        hi.append(z_ref[pl.ds(1, m, stride=2), :].astype(bf16))
    return jnp.concatenate(lo + hi, axis=1)


def _row_tiled(rows, width):
    return (rows // SUBLANES, width // LANES, SUBLANES, LANES)


def _row_of(ref, r):
    return ref.at[lax.shift_right_logical(r, LOG2_SUBLANES), :, pl.ds(r & (SUBLANES - 1), 1), :]


def _route_rows(h, g, w_ref, b, xp_ref, z_ref, earlier, base, n_experts):
    epg = n_experts // N_GROUPS
    xn = _rms_scale(h) * g
    _pack_rows(xn, xp_ref, z_ref)

    xh = xn.astype(bf16)
    xl = (xn - xh.astype(f32)).astype(bf16)
    hh_hl = jnp.dot(xh, w_ref[...], preferred_element_type=f32)
    logits = (hh_hl[:, :LANES] + hh_hl[:, LANES:]
              + jnp.dot(xl, w_ref[:, :LANES], preferred_element_type=f32) + b)
    lane = lax.broadcasted_iota(jnp.int32, logits.shape, 1)
    neg = jnp.float32(-jnp.inf)
    big = jnp.int32(LANES)

    gl = jnp.where(lane < N_GROUPS, logits, neg)
    gmax = jnp.max(gl, axis=-1, keepdims=True)
    g_sel = jnp.min(jnp.where(gl == gmax, lane, big), axis=-1, keepdims=True)
    p_top = 1.0 / jnp.sum(jnp.exp(gl - gmax), axis=-1, keepdims=True)

    lo_lane = N_GROUPS + g_sel * epg
    el = jnp.where((lane >= lo_lane) & (lane < lo_lane + epg), logits, neg)
    v1 = jnp.max(el, axis=-1, keepdims=True)
    l1 = jnp.min(jnp.where(el == v1, lane, big), axis=-1, keepdims=True)
    el2 = jnp.where(lane == l1, neg, el)
    v2 = jnp.max(el2, axis=-1, keepdims=True)
    l2 = jnp.min(jnp.where(el2 == v2, lane, big), axis=-1, keepdims=True)
    r = jnp.exp(v2 - v1)
    gate1 = p_top * (1.0 / (1.0 + r))
    gate2 = p_top * (r / (1.0 + r))

    onehot = ((lane == l1) | (lane == l2)).astype(f32)
    before = jnp.dot(earlier, onehot.astype(bf16), preferred_element_type=f32) + base
    rank1 = jnp.sum(jnp.where(lane == l1, before, 0.0), axis=-1, keepdims=True)
    rank2 = jnp.sum(jnp.where(lane == l2, before, 0.0), axis=-1, keepdims=True)
    base = base + jnp.sum(onehot, axis=0, keepdims=True)

    e1 = jnp.clip(l1 - N_GROUPS, 0, n_experts - 1)
    e2 = jnp.clip(l2 - N_GROUPS, 0, n_experts - 1)
    info = jnp.where(lane == 0, e1, 0)
    info = jnp.where(lane == 1, e2, info)
    info = jnp.where(lane == 2, rank1.astype(jnp.int32), info)
    info = jnp.where(lane == 3, rank2.astype(jnp.int32), info)
    info = jnp.where(lane == 4, pltpu.bitcast(jnp.broadcast_to(gate1, logits.shape), jnp.int32), info)
    info = jnp.where(lane == 5, pltpu.bitcast(jnp.broadcast_to(gate2, logits.shape), jnp.int32), info)
    return info, base


def _router_kernel(h_ref, g_ref, w_ref, b_ref, xp_ref, info_ref, infot_ref, cnt_ref, base_ref,
                   z_ref, *, n_experts):
    tm = h_ref.shape[0]

    @pl.when(pl.program_id(0) == 0)
    def _():
        base_ref[...] = jnp.zeros(base_ref.shape, f32)

    r_i = lax.broadcasted_iota(jnp.int32, (tm, tm), 0)
    c_i = lax.broadcasted_iota(jnp.int32, (tm, tm), 1)
    earlier = jnp.where(c_i < r_i, 1.0, 0.0).astype(bf16)
    info, base = _route_rows(h_ref[...], g_ref[...], w_ref, b_ref[...], xp_ref, z_ref, earlier,
                             base_ref[...], n_experts)
    info_ref[...] = info
    infot_ref[...] = info.T[0:INFO_ROWS, :]
    base_ref[...] = base
    cnt_ref[...] = base.astype(jnp.int32)


def _router(h, g, w_r, b_r, *, n_experts, tm):
    rows, d = h.shape
    kern = functools.partial(_router_kernel, n_experts=n_experts)
    return pl.pallas_call(
        kern,
        grid=(rows // tm,),
        in_specs=[
            pl.BlockSpec((tm, d), lambda i: (i, 0)),
            pl.BlockSpec((1, d), lambda i: (0, 0)),
            pl.BlockSpec((d, 2 * LANES), lambda i: (0, 0)),
            pl.BlockSpec((1, LANES), lambda i: (0, 0)),
        ],
        out_specs=[
            pl.BlockSpec(_row_tiled(tm, d // 2), lambda i: (i, 0, 0, 0)),
            pl.BlockSpec((tm, LANES), lambda i: (i, 0)),
            pl.BlockSpec((INFO_ROWS, tm), lambda i: (0, i)),
            pl.BlockSpec((1, LANES), lambda i: (0, 0)),
        ],
        out_shape=[
            jax.ShapeDtypeStruct(_row_tiled(rows, d // 2), jnp.uint32),
            jax.ShapeDtypeStruct((rows, LANES), jnp.int32),
            jax.ShapeDtypeStruct((INFO_ROWS, rows), jnp.int32),
            jax.ShapeDtypeStruct((1, LANES), jnp.int32),
        ],
        scratch_shapes=[pltpu.VMEM((1, LANES), f32), pltpu.VMEM((2 * tm, LANES), f32)],
        compiler_params=_cparams(("arbitrary",)),
        name="moe_router",
    )(h, g, w_r, b_r)


def _matmul_residual_kernel(a_ref, w_ref, r_ref, o_ref):
    o_ref[...] = r_ref[...] + jnp.dot(a_ref[...], w_ref[...], preferred_element_type=f32)


def _matmul_residual_split_kernel(a_ref, at_ref, w_ref, r_ref, rt_ref, o_ref, *, n_main):
    @pl.when(pl.program_id(0) < n_main)
    def _():
        o_ref[...] = r_ref[...] + jnp.dot(a_ref[...], w_ref[...], preferred_element_type=f32)

    @pl.when(pl.program_id(0) >= n_main)
    def _():
        o_ref[...] = rt_ref[...] + jnp.dot(at_ref[...], w_ref[...], preferred_element_type=f32)


def _matmul_residual(a, w, r, *, tm):
    rows, k = a.shape
    n = w.shape[1]
    return pl.pallas_call(
        _matmul_residual_kernel,
        grid=(rows // tm,),
        in_specs=[pl.BlockSpec((tm, k), lambda i: (i, 0)),
                  pl.BlockSpec((k, n), lambda i: (0, 0)),
                  pl.BlockSpec((tm, n), lambda i: (i, 0))],
        out_specs=pl.BlockSpec((tm, n), lambda i: (i, 0)),
        out_shape=jax.ShapeDtypeStruct((rows, n), f32),
        compiler_params=_cparams(("arbitrary",)),
        name="out_proj_residual",
    )(a, w, r)


def _matmul_residual_split(a, a_tail, w, r, r_tail, *, tm):
    rows, k = a.shape
    n = w.shape[1]
    n_main = rows // tm
    assert rows % tm == 0 and a_tail.shape[0] == tm and r.shape[0] == rows and r_tail.shape[0] == tm
    kern = functools.partial(_matmul_residual_split_kernel, n_main=n_main)

    def main_blk(i):
        return (jnp.minimum(i, n_main - 1), 0)

    return pl.pallas_call(
        kern,
        grid=(n_main + 1,),
        in_specs=[pl.BlockSpec((tm, k), main_blk),
                  pl.BlockSpec((tm, k), lambda i: (0, 0)),
                  pl.BlockSpec((k, n), lambda i: (0, 0)),
                  pl.BlockSpec((tm, n), main_blk),
                  pl.BlockSpec((tm, n), lambda i: (0, 0))],
        out_specs=pl.BlockSpec((tm, n), lambda i: (i, 0)),
        out_shape=jax.ShapeDtypeStruct((rows + tm, n), f32),
        compiler_params=_cparams(("arbitrary",)),
        name="out_proj_residual",
    )(a, a_tail, w, r, r_tail)


def _pad_fill_copies(fs_ref, pad_ref, zbuf, xs_ref, sem, e):
    fs = fs_ref[e]
    pad = pad_ref[e]
    head = pad & (SUBLANES - 1)
    out = []
    for k in range(SUBLANES - 1):
        out.append((k < head,
                    pltpu.make_async_copy(zbuf.at[0, :, pl.ds(0, 1), :], _row_of(xs_ref, fs + k), sem)))
    cur = lax.shift_right_logical(fs + head, LOG2_SUBLANES)
    for b in reversed(range(SUBLANES.bit_length() - 1, MOE_TILE.bit_length() - 1)):
        size = 1 << b
        groups = size // SUBLANES
        out.append(((pad & size) != 0,
                    pltpu.make_async_copy(zbuf.at[pl.ds(0, groups)], xs_ref.at[pl.ds(cur, groups)], sem)))
        cur = cur + lax.shift_right_logical(pad & size, LOG2_SUBLANES)
    return out


def _unused_tile_copies(zbuf, xs_ref, sem, t):
    groups = zbuf.shape[0]
    per_tile = MOE_TILE // SUBLANES
    return [pltpu.make_async_copy(zbuf, xs_ref.at[pl.ds(t * per_tile + k * groups, groups)], sem)
            for k in range(per_tile // groups)]


def _dispatch_kernel(p0_ref, p1_ref, fs_ref, pad_ref, nu_ref, x_ref, xs_ref, zbuf, sem, zsem, *,
                     n_steps, n_experts, max_tiles):
    i = pl.program_id(0)
    tm = x_ref.shape[0] * SUBLANES
    base = i * tm

    @pl.when(i == 0)
    def _():
        zbuf[...] = jnp.zeros(zbuf.shape, zbuf.dtype)

        def fill(e, c):
            for needed, copy in _pad_fill_copies(fs_ref, pad_ref, zbuf, xs_ref, zsem, e):
                @pl.when(needed)
                def _():
                    copy.start()
            return c

        lax.fori_loop(0, n_experts, fill, 0)

        def fill_unused(t, c):
            for copy in _unused_tile_copies(zbuf, xs_ref, zsem, t):
                copy.start()
            return c

        lax.fori_loop(nu_ref[0], max_tiles, fill_unused, 0)

    def issue(g, c):
        for k in range(SUBLANES):
            src = x_ref.at[g, :, pl.ds(k, 1), :]
            r = base + g * SUBLANES + k
            pltpu.make_async_copy(src, _row_of(xs_ref, p0_ref[r]), sem).start(priority=0)
            pltpu.make_async_copy(src, _row_of(xs_ref, p1_ref[r]), sem).start(priority=1)
        return c

    lax.fori_loop(0, tm // SUBLANES, issue, 0)
    for _ in range(TOP_K):
        pltpu.make_async_copy(x_ref, xs_ref.at[pl.ds(0, tm // SUBLANES)], sem).wait()

    @pl.when(i == n_steps - 1)
    def _():
        def drain(e, c):
            for needed, copy in _pad_fill_copies(fs_ref, pad_ref, zbuf, xs_ref, zsem, e):
                @pl.when(needed)
                def _():
                    copy.wait()
            return c

        lax.fori_loop(0, n_experts, drain, 0)

        def drain_unused(t, c):
            for copy in _unused_tile_copies(zbuf, xs_ref, zsem, t):
                copy.wait()
            return c

        lax.fori_loop(nu_ref[0], max_tiles, drain_unused, 0)


def _dispatch(pos0, pos1, fill_start, pad, n_used, xp, *, sorted_rows, tm):
    width = xp.shape[1] * LANES
    n_steps = (xp.shape[0] * SUBLANES) // tm
    kern = functools.partial(_dispatch_kernel, n_steps=n_steps, n_experts=pad.shape[0],
                             max_tiles=sorted_rows // MOE_TILE)
    return pl.pallas_call(
        kern,
        grid_spec=pltpu.PrefetchScalarGridSpec(
            num_scalar_prefetch=5,
            grid=(n_steps,),
            in_specs=[pl.BlockSpec(_row_tiled(tm, width), lambda i, *_: (i, 0, 0, 0))],
            out_specs=pl.BlockSpec(memory_space=pl.ANY),
            scratch_shapes=[pltpu.VMEM(_row_tiled(MOE_TILE // 2, width), jnp.uint32),
                            pltpu.SemaphoreType.DMA(()), pltpu.SemaphoreType.DMA(())],
        ),
        out_shape=jax.ShapeDtypeStruct(_row_tiled(sorted_rows, width), jnp.uint32),
        compiler_params=pltpu.CompilerParams(dimension_semantics=("arbitrary",),
                                             vmem_limit_bytes=VMEM_LIMIT, has_side_effects=True),
        name="moe_dispatch",
    )(pos0, pos1, fill_start, pad, n_used, xp)


def _experts_kernel(te_ref, tf_ref, nx_ref, nu_ref, xs_ref, wg_hbm, wu_hbm, wd_hbm, y_ref,
                    wg_f, wu_f, wd_f, wg_b, wu_b, wd_b, z_ref, slot_ref, sems, *, layer):
    t = pl.program_id(0)

    def weight_copies(e, slot):
        return [pltpu.make_async_copy(hbm.at[layer, e], buf.at[slot], sems.at[slot, k])
                for k, (hbm, buf) in enumerate(((wg_hbm, wg_f), (wu_hbm, wu_f), (wd_hbm, wd_f)))]

    @pl.when(t == 0)
    def _():
        slot_ref[0] = 0
        for c in weight_copies(te_ref[0], 0):
            c.start()

    @pl.when(tf_ref[t] == 1)
    def _():
        slot = slot_ref[0]
        for c in weight_copies(te_ref[t], slot):
            c.wait()

        @pl.when(nx_ref[t] >= 0)
        def _():
            for c in weight_copies(nx_ref[t], 1 - slot):
                c.start()

        wg_b[...] = wg_f[slot].astype(bf16)
        wu_b[...] = wu_f[slot].astype(bf16)
        wd_b[...] = wd_f[slot].astype(bf16)
        slot_ref[0] = 1 - slot

    @pl.when(t < nu_ref[0])
    def _():
        x = _unpack_rows(xs_ref, z_ref)
        gate = jnp.dot(x, wg_b[...], preferred_element_type=f32)
        up = jnp.dot(x, wu_b[...], preferred_element_type=f32)
        hidden = (gate * (1.0 / (1.0 + jnp.exp(-gate))) * up).astype(bf16)
        _pack_rows(jnp.dot(hidden, wd_b[...], preferred_element_type=f32), y_ref, z_ref)

    @pl.when(t >= nu_ref[0])
    def _():
        y_ref[...] = jnp.zeros(y_ref.shape, y_ref.dtype)


def _experts(tile_expert, tile_first, next_expert, n_used, xs, w_gate, w_up, w_down, *, layer,
             max_tiles):
    tm = MOE_TILE
    _, _, d, f = w_gate.shape
    kern = functools.partial(_experts_kernel, layer=layer)
    hbm = pl.BlockSpec(memory_space=pl.ANY)
    return pl.pallas_call(
        kern,
        grid_spec=pltpu.PrefetchScalarGridSpec(
            num_scalar_prefetch=4,
            grid=(max_tiles,),
            in_specs=[pl.BlockSpec(_row_tiled(tm, d // 2),
                                   lambda t, te, tf, nx, nu: (jnp.clip(nu[0] - 1, 0, t), 0, 0, 0)),
                      hbm, hbm, hbm],
            out_specs=pl.BlockSpec(_row_tiled(tm, d // 2), lambda t, *_: (t, 0, 0, 0)),
            scratch_shapes=[pltpu.VMEM((2, d, f), f32), pltpu.VMEM((2, d, f), f32),
                            pltpu.VMEM((2, f, d), f32),
                            pltpu.VMEM((d, f), bf16), pltpu.VMEM((d, f), bf16),
                            pltpu.VMEM((f, d), bf16), pltpu.VMEM((2 * tm, LANES), f32),
                            pltpu.SMEM((1,), jnp.int32), pltpu.SemaphoreType.DMA((2, 3))],
        ),
        out_shape=jax.ShapeDtypeStruct(_row_tiled(max_tiles * tm, d // 2), jnp.uint32),
        compiler_params=_cparams(("arbitrary",)),
        name="moe_experts",
    )(tile_expert, tile_first, next_expert, n_used, xs, w_gate, w_up, w_down)


def _combine_kernel(p0_ref, p1_ref, h_ref, info_ref, gf_ref, ys_ref, o_ref, buf, z_ref, sems, *,
                    n_steps, final_norm):
    i = pl.program_id(0)
    tm = h_ref.shape[0]

    def issue(step, slot):
        base = step * tm

        def body(g, c):
            for k in range(SUBLANES):
                r = base + g * SUBLANES + k
                pltpu.make_async_copy(_row_of(ys_ref, p0_ref[r]),
                                      buf.at[slot, 0, g, :, pl.ds(k, 1), :],
                                      sems.at[slot]).start(priority=0)
                pltpu.make_async_copy(_row_of(ys_ref, p1_ref[r]),
                                      buf.at[slot, 1, g, :, pl.ds(k, 1), :],
                                      sems.at[slot]).start(priority=1)
            return c

        lax.fori_loop(0, tm // SUBLANES, body, 0)

    @pl.when(i == 0)
    def _():
        issue(0, 0)

    @pl.when(i + 1 < n_steps)
    def _():
        issue(i + 1, lax.rem(i + 1, 2))

    slot = lax.rem(i, 2)
    for k in range(TOP_K):
        pltpu.make_async_copy(ys_ref.at[pl.ds(0, tm // SUBLANES)], buf.at[slot, k],
                              sems.at[slot]).wait()

    info = info_ref[...]
    lane = lax.broadcasted_iota(jnp.int32, info.shape, 1)
    gates = pltpu.bitcast(info, f32)
    g1 = jnp.sum(jnp.where(lane == 4, gates, 0.0), axis=-1, keepdims=True)
    g2 = jnp.sum(jnp.where(lane == 5, gates, 0.0), axis=-1, keepdims=True)
    lo, hi = [], []
    for c in range(buf.shape[3]):
        for k in range(TOP_K):
            words = buf[slot, k, :, c, :, :].reshape(tm, LANES)
            z_ref[k] = pltpu.bitcast(words, bf16).astype(f32)
        lo.append(g1 * z_ref[0, pl.ds(0, tm, stride=2), :] + g2 * z_ref[1, pl.ds(0, tm, stride=2), :])
        hi.append(g1 * z_ref[0, pl.ds(1, tm, stride=2), :] + g2 * z_ref[1, pl.ds(1, tm, stride=2), :])
    out = h_ref[...] + jnp.concatenate(lo + hi, axis=1)
    if final_norm:
        out = _rms_scale(out) * gf_ref[...]
    o_ref[...] = out


def _combine(pos0, pos1, h, info, gf, ys, *, out_rows, tm, final_norm):
    d = h.shape[1]
    n_steps = out_rows // tm
    kern = functools.partial(_combine_kernel, n_steps=n_steps, final_norm=final_norm)
    return pl.pallas_call(
        kern,
        grid_spec=pltpu.PrefetchScalarGridSpec(
            num_scalar_prefetch=2,
            grid=(n_steps,),
            in_specs=[
                pl.BlockSpec((tm, d), lambda i, p0, p1: (i, 0)),
                pl.BlockSpec((tm, LANES), lambda i, p0, p1: (i, 0)),
                pl.BlockSpec((1, d), lambda i, p0, p1: (0, 0)),
                pl.BlockSpec(memory_space=pl.ANY),
            ],
            out_specs=pl.BlockSpec((tm, d), lambda i, p0, p1: (i, 0)),
            scratch_shapes=[pltpu.VMEM((2, TOP_K) + _row_tiled(tm, d // 2), jnp.uint32),
                            pltpu.VMEM((TOP_K, 2 * tm, LANES), f32),
                            pltpu.SemaphoreType.DMA((2,))],
        ),
        out_shape=jax.ShapeDtypeStruct((out_rows, d), f32),
        compiler_params=_cparams(("arbitrary",)),
        name="moe_combine",
    )(pos0, pos1, h, info, gf, ys)


def _table_lookup(table, idx):
    k = lax.broadcasted_iota(jnp.int32, (table.shape[0], idx.shape[0]), 0)
    return jnp.sum(jnp.where(k == idx[None, :], table[:, None], 0), axis=0)


def _router_params(wg_r, bg_r, we_r, be_r):
    d, n_experts = we_r.shape
    n_fill = LANES - N_GROUPS - n_experts
    w_r = jnp.concatenate([wg_r, we_r, jnp.zeros((d, n_fill), f32)], axis=1)
    b_r = jnp.concatenate([bg_r, be_r, jnp.zeros((n_fill,), f32)]).reshape(1, LANES)
    w_hi = w_r.astype(bf16)
    return jnp.concatenate([w_hi, (w_r - w_hi.astype(f32)).astype(bf16)], axis=1), b_r


def _mixer_out_and_moe(a, a_tail, w_out, r, r_tail, g_ffn, wg_r, bg_r, we_r, be_r, w_gate, w_up,
                       w_down, gf, *, layer, out_rows, final_norm):
    d = w_out.shape[1]
    n_experts = we_r.shape[1]
    w_hl, b_r = _router_params(wg_r, bg_r, we_r, be_r)
    if a_tail is None:
        h = _matmul_residual(a, w_out, r, tm=TAIL_ROWS)
    else:
        h = _matmul_residual_split(a, a_tail, w_out, r, r_tail, tm=TAIL_ROWS)
    rows = h.shape[0]
    xp, info, info_t, cnt = _router(h, g_ffn.reshape(1, d), w_hl, b_r, n_experts=n_experts,
                                    tm=TAIL_ROWS)

    tm = MOE_TILE
    max_tiles = (TOP_K * rows) // tm + n_experts
    i32 = jnp.int32
    cnt = cnt[0, N_GROUPS:N_GROUPS + n_experts]
    n_tiles = (cnt + tm - 1) // tm
    tile_end = jnp.cumsum(n_tiles)
    tile_start = tile_end - n_tiles
    n_used = tile_end[-1]
    t_ids = jnp.arange(max_tiles, dtype=i32)[None, :]
    e_ids = jnp.arange(n_experts, dtype=i32)[:, None]
    first_t, end_t = tile_start[:, None], tile_end[:, None]
    member = (first_t <= t_ids) & (t_ids < end_t)
    te = jnp.sum(jnp.where(member, e_ids, 0), axis=0)
    tile_first = jnp.any(member & (first_t == t_ids), axis=0).astype(i32)
    nxt_tile = jnp.sum(jnp.where(member, end_t, 0), axis=0)[None, :]
    is_next = (first_t == nxt_tile) & (n_tiles[:, None] > 0)
    next_expert = jnp.where(jnp.any(is_next, axis=0), jnp.sum(jnp.where(is_next, e_ids, 0), axis=0), -1)

    sorted_rows = max_tiles * tm
    row_off = (tile_start * tm).astype(i32)
    pos0 = jnp.clip(_table_lookup(row_off, info_t[0]) + info_t[2], 0, sorted_rows - 1)
    pos1 = jnp.clip(_table_lookup(row_off, info_t[1]) + info_t[3], 0, sorted_rows - 1)

    n_used = n_used.reshape(1).astype(i32)
    xs = _dispatch(pos0, pos1, (row_off + cnt).astype(i32), (n_tiles * tm - cnt).astype(i32), n_used,
                   xp, sorted_rows=sorted_rows, tm=DISPATCH_ROWS)
    ys = _experts(te.astype(i32), tile_first, next_expert.astype(i32), n_used,
                  xs, w_gate, w_up, w_down, layer=layer, max_tiles=max_tiles)
    return _combine(pos0, pos1, h, info, gf.reshape(1, d), ys, out_rows=out_rows, tm=COMBINE_ROWS,
                    final_norm=final_norm)


def _strict_upper_ones(n):
    r = lax.broadcasted_iota(jnp.int32, (n, n), 0)
    c = lax.broadcasted_iota(jnp.int32, (n, n), 1)
    return (c > r).astype(bf16)


def kernel(x, meta_tokens, norm_mix_g, norm_ffn_g, conv_w_in, conv_w, conv_w_out, attn_w_qkv, attn_w_out, router_group_w, router_group_b, router_expert_w, router_expert_b, moe_w_gate, moe_w_up, moe_w_down, final_norm_g):
    batch, seq_len, d = x.shape
    n_meta = meta_tokens.shape[0]
    n_pad = META_BLOCK - n_meta
    n_tok = batch * seq_len
    assert seq_len % TAIL_ROWS == 0 and d % (2 * N_HEADS) == 0

    x2d = x.reshape(n_tok, d)
    tail = jnp.concatenate([jnp.zeros((n_pad, d), f32), meta_tokens.astype(f32),
                            jnp.zeros((TAIL_ROWS - META_BLOCK, d), f32)], axis=0)

    tc = _largest_divisor(d, CONV_COL_TILES)
    g_mix0 = norm_mix_g[0].reshape(1, d)
    w_in = conv_w_in[0].astype(bf16)
    gated_tail, meta_v = _conv_in(tail, g_mix0, w_in, conv_w[0], jnp.zeros((SUBLANES, d), f32),
                                  seq_len=TAIL_ROWS, tm=TAIL_ROWS, tc=tc)
    gated, _ = _conv_in(x2d, g_mix0, w_in, conv_w[0], meta_v, seq_len=seq_len,
                        tm=_largest_divisor(seq_len, CONV_ROW_TILES), tc=tc)
    h = _mixer_out_and_moe(gated, gated_tail, conv_w_out[0].astype(bf16), x2d, tail, norm_ffn_g[0],
                           router_group_w[0], router_group_b[0], router_expert_w[0],
                           router_expert_b[0], moe_w_gate, moe_w_up, moe_w_down, final_norm_g,
                           layer=0, out_rows=n_tok + TAIL_ROWS, final_norm=False)

    qkv = _norm_matmul(h, norm_mix_g[1].reshape(1, d), attn_w_qkv[0].astype(bf16),
                       tm=_largest_divisor(h.shape[0], QKV_ROW_TILES),
                       tn=_largest_divisor(d, QKV_COL_TILES),
                       scaled_cols=d, scale=(d // N_HEADS) ** -0.5)
    o = _attention(qkv, _strict_upper_ones(ATT_BLOCK), batch=batch, seq_len=seq_len, d_model=d,
                   n_pad=n_pad)
    out = _mixer_out_and_moe(o, None, attn_w_out[0].astype(bf16), h, None, norm_ffn_g[1],
                             router_group_w[1], router_group_b[1], router_expert_w[1],
                             router_expert_b[1], moe_w_gate, moe_w_up, moe_w_down, final_norm_g,
                             layer=1, out_rows=n_tok, final_norm=True)
    return out.reshape(batch, seq_len, d)
```

```python
import functools

import jax
import jax.numpy as jnp
from jax import lax
from jax.experimental import pallas as pl
from jax.experimental.pallas import tpu as pltpu

N_HEADS = 16
N_GROUPS = 4
TOP_K = 2
CONV_WIDTH = 3
EPS = 1e-6

LANES = 128
SUBLANES = 8
LOG2_SUBLANES = SUBLANES.bit_length() - 1
META_BLOCK = 128
TAIL_ROWS = 512
MOE_TILE = 256
ATT_BLOCK = 256
ATT_HEADS = 8
ATT_QSUB = 2
INFO_ROWS = 8
CONV_ROW_TILES = (1024, 512)
CONV_COL_TILES = (512, 256)
QKV_ROW_TILES = (1536, 512)
QKV_COL_TILES = (1024, 512, 256, 128)
DISPATCH_ROWS = 512
COMBINE_ROWS = 256
SKIP_BELOW = -104.0
LOG2E = 1.4426950408889634
VMEM_LIMIT = 56 * 1024 * 1024

f32 = jnp.float32
bf16 = jnp.bfloat16


def _cparams(sem):
    return pltpu.CompilerParams(dimension_semantics=sem, vmem_limit_bytes=VMEM_LIMIT)


def _largest_divisor(n, candidates):
    return next(c for c in candidates if n % c == 0)


def _rms_scale(x):
    return x * lax.rsqrt(jnp.mean(x * x, axis=-1, keepdims=True) + EPS)


def _norm_matmul_kernel(h_ref, g_ref, w_ref, o_ref, xn_ref, *, n_scaled, scale):
    j = pl.program_id(1)

    @pl.when(j == 0)
    def _():
        xn_ref[...] = (_rms_scale(h_ref[...]) * g_ref[...]).astype(bf16)

    mult = jnp.where(j < n_scaled, scale, 1.0)
    o_ref[...] = (jnp.dot(xn_ref[...], w_ref[...], preferred_element_type=f32) * mult).astype(o_ref.dtype)


def _norm_matmul(h, g, w, *, tm, tn, scaled_cols, scale):
    rows, d = h.shape
    n = w.shape[1]
    assert scaled_cols % tn == 0
    kern = functools.partial(_norm_matmul_kernel, n_scaled=scaled_cols // tn, scale=scale)
    return pl.pallas_call(
        kern,
        grid=(rows // tm, n // tn),
        in_specs=[
            pl.BlockSpec((tm, d), lambda i, j: (i, 0)),
            pl.BlockSpec((1, d), lambda i, j: (0, 0)),
            pl.BlockSpec((d, tn), lambda i, j: (0, j)),
        ],
        out_specs=pl.BlockSpec((tm, tn), lambda i, j: (i, j)),
        out_shape=jax.ShapeDtypeStruct((rows, n), bf16),
        scratch_shapes=[pltpu.VMEM((tm, d), bf16)],
        compiler_params=_cparams(("arbitrary", "arbitrary")),
        name="norm_qkv",
    )(h, g, w)


def _conv_in_kernel(x_ref, g_ref, wb_ref, wc_ref, wh_ref, cw_ref, lead_ref, o_ref, meta_v_ref,
                    xn_ref, carry_ref, *, tiles_per_seq):
    i = pl.program_id(0)
    j = pl.program_id(1)
    tm, tc = o_ref.shape

    @pl.when(j == 0)
    def _():
        xn_ref[...] = (_rms_scale(x_ref[...]) * g_ref[...]).astype(bf16)

    xn = xn_ref[...]
    seq_start = lax.rem(i, tiles_per_seq) == 0
    row = lax.broadcasted_iota(jnp.int32, (tm, tc // 2), 0)
    halves = (slice(0, tc // 2), slice(tc // 2, tc))
    projected = [tuple(jnp.dot(xn, w[:, cols], preferred_element_type=f32)
                       for w in (wb_ref, wc_ref, wh_ref)) for cols in halves]
    for cols, (b_gate, c_gate, hh) in zip(halves, projected):
        v = c_gate * hh
        prev = jnp.where(seq_start, lead_ref[:, cols], carry_ref[j, :, cols])
        p1 = prev[SUBLANES - 1:SUBLANES, :]
        p2 = prev[SUBLANES - 2:SUBLANES - 1, :]
        s1 = jnp.where(row == 0, p1, pltpu.roll(v, 1, 0))
        s2 = jnp.where(row == 0, p2, jnp.where(row == 1, p1, pltpu.roll(v, 2, 0)))
        y = cw_ref[0:1, cols] * s2 + cw_ref[1:2, cols] * s1 + cw_ref[2:3, cols] * v
        o_ref[:, cols] = (b_gate * y).astype(o_ref.dtype)
        carry_ref[j, :, cols] = v[tm - SUBLANES:, :]
        meta_v_ref[:, cols] = v[META_BLOCK - SUBLANES:META_BLOCK, :]


def _conv_in(x2d, g, w_in, conv_w, lead, *, seq_len, tm, tc):
    rows, d = x2d.shape
    n_j = d // tc
    assert seq_len % tm == 0 and rows % tm == 0
    kern = functools.partial(_conv_in_kernel, tiles_per_seq=seq_len // tm)
    return pl.pallas_call(
        kern,
        grid=(rows // tm, n_j),
        in_specs=[
            pl.BlockSpec((tm, d), lambda i, j: (i, 0)),
            pl.BlockSpec((1, d), lambda i, j: (0, 0)),
            pl.BlockSpec((d, tc), lambda i, j: (0, j)),
            pl.BlockSpec((d, tc), lambda i, j: (0, n_j + j)),
            pl.BlockSpec((d, tc), lambda i, j: (0, 2 * n_j + j)),
            pl.BlockSpec((CONV_WIDTH, tc), lambda i, j: (0, j)),
            pl.BlockSpec((SUBLANES, tc), lambda i, j: (0, j)),
        ],
        out_specs=[pl.BlockSpec((tm, tc), lambda i, j: (i, j)),
                   pl.BlockSpec((SUBLANES, tc), lambda i, j: (i, j))],
        out_shape=[jax.ShapeDtypeStruct((rows, d), bf16),
                   jax.ShapeDtypeStruct((SUBLANES * (rows // tm), d), f32)],
        scratch_shapes=[pltpu.VMEM((tm, d), bf16), pltpu.VMEM((n_j, SUBLANES, tc), f32)],
        compiler_params=_cparams(("arbitrary", "arbitrary")),
        name="conv_in",
    )(x2d, g, w_in, w_in, w_in, conv_w, lead)


def _sb_scores(q, kb):
    return lax.dot_general(kb, q, (((1,), (1,)), ((), ())), preferred_element_type=f32)


def _sb_logs(z, mask):
    softplus_neg = jnp.log(1.0 + jnp.exp2(jnp.abs(z) * (-LOG2E)))
    log_sig = jnp.minimum(z, 0.0) - softplus_neg
    log_1m = log_sig - z
    if mask is not None:
        log_1m = jnp.where(mask, log_1m, 0.0)
    return log_sig, jnp.sum(log_1m, axis=0, keepdims=True), log_1m.astype(bf16)


def _sb_suffix_sums(u, log_1m):
    return jnp.dot(u, log_1m, preferred_element_type=f32)


def _sb_weights(log_sig, log_1m_sum, rest, carry, mask):
    a = jnp.exp2((log_sig + rest + carry) * LOG2E)
    if mask is not None:
        a = jnp.where(mask, a, 0.0)
    return a.astype(bf16), carry + log_1m_sum


def _sb_values(vb, a):
    return lax.dot_general(vb, a, (((0,), (0,)), ((), ())), preferred_element_type=f32)


def _sb_block(q, kb, vb, u, carry, mask):
    log_sig, total, log_1m = _sb_logs(_sb_scores(q, kb), mask)
    a, carry = _sb_weights(log_sig, total, _sb_suffix_sums(u, log_1m), carry, mask)
    return _sb_values(vb, a), carry


def _attn_kernel(q_ref, k_ref, v_ref, km_ref, vm_ref, u_ref, o_ref, acc_ref, carry_ref, *,
                 n_pad, n_main_steps):
    step = pl.program_id(1)
    steps_per_seq = k_ref.shape[0] // (ATT_QSUB * ATT_BLOCK)
    qstep = lax.rem(step, steps_per_seq)
    tb = ATT_BLOCK
    dh = q_ref.shape[1] // ATT_HEADS
    u = u_ref[...]
    s_loc = lax.broadcasted_iota(jnp.int32, (tb, tb), 0)
    t_loc = lax.broadcasted_iota(jnp.int32, (tb, tb), 1)
    causal = s_loc < t_loc
    meta_keys = (s_loc >= n_pad) & (s_loc < META_BLOCK)

    @pl.when(step < n_main_steps)
    def _():
        chains = [(hh, qs) for hh in range(ATT_HEADS) for qs in range(ATT_QSUB)]
        first = qstep == 0

        def keys(ref, mref, hh, qs, prev):
            cols = slice(hh * dh, (hh + 1) * dh)
            i = qstep * ATT_QSUB + qs
            blk = jnp.maximum(i - 1, 0) if prev else i
            kv = ref[pl.ds(pl.multiple_of(blk * tb, tb), tb), cols]
            if prev and qs == 0:
                kv = jnp.where(first, mref[:, cols], kv)
            return kv

        items = [(hh, qs, False) for hh, qs in chains] + [(hh, qs, True) for hh, qs in chains]
        n_items = len(items)
        masks = [(meta_keys | jnp.logical_not(first)) if (prev and qs == 0)
                 else (None if prev else causal) for hh, qs, prev in items]
        z, logs, rest, wts, contrib, carry_out = {}, {}, {}, {}, {}, {}
        for t in range(n_items + 4):
            if 0 <= t - 4 < n_items:
                n = t - 4
                hh, qs, prev = items[n]
                contrib[n] = _sb_values(keys(v_ref, vm_ref, hh, qs, prev), wts.pop(n))
            if 0 <= t - 3 < n_items:
                n = t - 3
                hh, qs, prev = items[n]
                carry_in = carry_out[n - len(chains)] if prev else jnp.zeros((1, tb), f32)
                log_sig, row0, _ = logs.pop(n)
                wts[n], carry_out[n] = _sb_weights(log_sig, row0, rest.pop(n), carry_in, masks[n])
            if 0 <= t - 2 < n_items:
                rest[t - 2] = _sb_suffix_sums(u, logs[t - 2][2])
            if 0 <= t - 1 < n_items:
                logs[t - 1] = _sb_logs(z.pop(t - 1), masks[t - 1])
            if t < n_items:
                hh, qs, prev = items[t]
                q = q_ref[qs * tb:(qs + 1) * tb, hh * dh:(hh + 1) * dh]
                z[t] = _sb_scores(q, keys(k_ref, km_ref, hh, qs, prev))
        for c, (hh, qs) in enumerate(chains):
            acc_ref[hh, qs] = contrib[c] + contrib[c + len(chains)]
            carry_ref[hh, qs] = carry_out[c + len(chains)]

        top_all = jnp.max(functools.reduce(
            jnp.maximum, [carry_out[c + len(chains)] for c in range(len(chains))]))

        @pl.when(top_all > SKIP_BELOW)
        def _():
            for hh, qs in chains:
                cols = slice(hh * dh, (hh + 1) * dh)
                i = qstep * ATT_QSUB + qs

                def cond(state):
                    j, top = state
                    return (j >= 0) & (top > SKIP_BELOW)

                def body(state, hh=hh, qs=qs, cols=cols):
                    j, _ = state
                    st = pl.multiple_of(j * tb, tb)
                    q = q_ref[qs * tb:(qs + 1) * tb, cols]
                    contrib, carry = _sb_block(q, k_ref[pl.ds(st, tb), cols],
                                               v_ref[pl.ds(st, tb), cols], u, carry_ref[hh, qs],
                                               None)
                    acc_ref[hh, qs] += contrib
                    carry_ref[hh, qs] = carry
                    return j - 1, jnp.max(carry)

                _, top = lax.while_loop(cond, body, (i - 2, jnp.max(carry_ref[hh, qs])))

                @pl.when((top > SKIP_BELOW) & (i >= 1))
                def _(hh=hh, qs=qs, cols=cols):
                    q = q_ref[qs * tb:(qs + 1) * tb, cols]
                    contrib, _ = _sb_block(q, km_ref[:, cols], vm_ref[:, cols], u,
                                           carry_ref[hh, qs], meta_keys)
                    acc_ref[hh, qs] += contrib

        for hh, qs in chains:
            o_ref[qs * tb:(qs + 1) * tb, hh * dh:(hh + 1) * dh] = acc_ref[hh, qs].T.astype(o_ref.dtype)

    @pl.when(step >= n_main_steps)
    def _():
        o_ref[...] = jnp.zeros(o_ref.shape, o_ref.dtype)
        for hh in range(ATT_HEADS):
            cols = slice(hh * dh, (hh + 1) * dh)
            contrib, _ = _sb_block(q_ref[0:tb, cols], km_ref[:, cols], vm_ref[:, cols], u,
                                   jnp.zeros((1, tb), f32), causal & meta_keys)
            o_ref[0:tb, cols] = contrib.T.astype(o_ref.dtype)


def _attention(qkv, u, *, batch, seq_len, d_model, n_pad):
    rows = qkv.shape[0]
    dh = d_model // N_HEADS
    tqs = ATT_QSUB * ATT_BLOCK
    assert tqs == TAIL_ROWS and seq_len % tqs == 0 and N_HEADS % ATT_HEADS == 0
    steps_per_seq = seq_len // tqs
    n_main = batch * steps_per_seq
    n_hp = N_HEADS // ATT_HEADS
    wcols = ATT_HEADS * dh
    meta_blk = (batch * seq_len) // ATT_BLOCK
    kern = functools.partial(_attn_kernel, n_pad=n_pad, n_main_steps=n_main)

    def seq_of(s):
        return jnp.minimum(s // steps_per_seq, batch - 1)

    return pl.pallas_call(
        kern,
        grid=(n_hp, n_main + 1),
        in_specs=[
            pl.BlockSpec((tqs, wcols), lambda h, s: (s, h)),
            pl.BlockSpec((seq_len, wcols), lambda h, s: (seq_of(s), n_hp + h)),
            pl.BlockSpec((seq_len, wcols), lambda h, s: (seq_of(s), 2 * n_hp + h)),
            pl.BlockSpec((ATT_BLOCK, wcols), lambda h, s: (meta_blk, n_hp + h)),
            pl.BlockSpec((ATT_BLOCK, wcols), lambda h, s: (meta_blk, 2 * n_hp + h)),
            pl.BlockSpec((ATT_BLOCK, ATT_BLOCK), lambda h, s: (0, 0)),
        ],
        out_specs=pl.BlockSpec((tqs, wcols), lambda h, s: (s, h)),
        out_shape=jax.ShapeDtypeStruct((rows, d_model), bf16),
        scratch_shapes=[pltpu.VMEM((ATT_HEADS, ATT_QSUB, dh, ATT_BLOCK), f32),
                        pltpu.VMEM((ATT_HEADS, ATT_QSUB, 1, ATT_BLOCK), f32)],
        compiler_params=_cparams(("arbitrary", "arbitrary")),
        name="sb_attention",
    )(qkv, qkv, qkv, qkv, qkv, u)


def _pack_rows(x, w_ref, z_ref):
    m, d = x.shape
    half = d // 2
    for c in range(half // LANES):
        z_ref[pl.ds(0, m, stride=2), :] = x[:, c * LANES:(c + 1) * LANES]
        z_ref[pl.ds(1, m, stride=2), :] = x[:, half + c * LANES:half + (c + 1) * LANES]
        words = pltpu.bitcast(z_ref[...].astype(bf16), jnp.uint32)
        w_ref[:, c, :, :] = words.reshape(m // SUBLANES, SUBLANES, LANES)


def _unpack_rows(w_ref, z_ref):
    m = w_ref.shape[0] * SUBLANES
    lo, hi = [], []
    for c in range(w_ref.shape[1]):
        words = w_ref[:, c, :, :].reshape(m, LANES)
        z_ref[...] = pltpu.bitcast(words, bf16).astype(f32)
        lo.append(z_ref[pl.ds(0, m, stride=2), :].astype(bf16))
        hi.append(z_ref[pl.ds(1, m, stride=2), :].astype(bf16))
    return jnp.concatenate(lo + hi, axis=1)


def _row_tiled(rows, width):
    return (rows // SUBLANES, width // LANES, SUBLANES, LANES)


def _row_of(ref, r):
    return ref.at[lax.shift_right_logical(r, LOG2_SUBLANES), :, pl.ds(r & (SUBLANES - 1), 1), :]


def _route_rows(h, g, w_ref, b, xp_ref, z_ref, earlier, base, n_experts):
    epg = n_experts // N_GROUPS
    xn = _rms_scale(h) * g
    _pack_rows(xn, xp_ref, z_ref)

    xh = xn.astype(bf16)
    xl = (xn - xh.astype(f32)).astype(bf16)
    hh_hl = jnp.dot(xh, w_ref[...], preferred_element_type=f32)
    logits = (hh_hl[:, :LANES] + hh_hl[:, LANES:]
              + jnp.dot(xl, w_ref[:, :LANES], preferred_element_type=f32) + b)
    lane = lax.broadcasted_iota(jnp.int32, logits.shape, 1)
    neg = jnp.float32(-jnp.inf)
    big = jnp.int32(LANES)

    gl = jnp.where(lane < N_GROUPS, logits, neg)
    gmax = jnp.max(gl, axis=-1, keepdims=True)
    g_sel = jnp.min(jnp.where(gl == gmax, lane, big), axis=-1, keepdims=True)
    p_top = 1.0 / jnp.sum(jnp.exp(gl - gmax), axis=-1, keepdims=True)

    lo_lane = N_GROUPS + g_sel * epg
    el = jnp.where((lane >= lo_lane) & (lane < lo_lane + epg), logits, neg)
    v1 = jnp.max(el, axis=-1, keepdims=True)
    l1 = jnp.min(jnp.where(el == v1, lane, big), axis=-1, keepdims=True)
    el2 = jnp.where(lane == l1, neg, el)
    v2 = jnp.max(el2, axis=-1, keepdims=True)
    l2 = jnp.min(jnp.where(el2 == v2, lane, big), axis=-1, keepdims=True)
    r = jnp.exp(v2 - v1)
    gate1 = p_top * (1.0 / (1.0 + r))
    gate2 = p_top * (r / (1.0 + r))

    onehot = ((lane == l1) | (lane == l2)).astype(f32)
    before = jnp.dot(earlier, onehot.astype(bf16), preferred_element_type=f32) + base
    rank1 = jnp.sum(jnp.where(lane == l1, before, 0.0), axis=-1, keepdims=True)
    rank2 = jnp.sum(jnp.where(lane == l2, before, 0.0), axis=-1, keepdims=True)
    base = base + jnp.sum(onehot, axis=0, keepdims=True)

    e1 = jnp.clip(l1 - N_GROUPS, 0, n_experts - 1)
    e2 = jnp.clip(l2 - N_GROUPS, 0, n_experts - 1)
    info = jnp.where(lane == 0, e1, 0)
    info = jnp.where(lane == 1, e2, info)
    info = jnp.where(lane == 2, rank1.astype(jnp.int32), info)
    info = jnp.where(lane == 3, rank2.astype(jnp.int32), info)
    info = jnp.where(lane == 4, pltpu.bitcast(jnp.broadcast_to(gate1, logits.shape), jnp.int32), info)
    info = jnp.where(lane == 5, pltpu.bitcast(jnp.broadcast_to(gate2, logits.shape), jnp.int32), info)
    return info, base


def _router_kernel(h_ref, g_ref, w_ref, b_ref, xp_ref, info_ref, infot_ref, cnt_ref, base_ref,
                   z_ref, *, n_experts):
    tm = h_ref.shape[0]

    @pl.when(pl.program_id(0) == 0)
    def _():
        base_ref[...] = jnp.zeros(base_ref.shape, f32)

    r_i = lax.broadcasted_iota(jnp.int32, (tm, tm), 0)
    c_i = lax.broadcasted_iota(jnp.int32, (tm, tm), 1)
    earlier = jnp.where(c_i < r_i, 1.0, 0.0).astype(bf16)
    info, base = _route_rows(h_ref[...], g_ref[...], w_ref, b_ref[...], xp_ref, z_ref, earlier,
                             base_ref[...], n_experts)
    info_ref[...] = info
    infot_ref[...] = info.T[0:INFO_ROWS, :]
    base_ref[...] = base
    cnt_ref[...] = base.astype(jnp.int32)


def _router(h, g, w_r, b_r, *, n_experts, tm):
    rows, d = h.shape
    kern = functools.partial(_router_kernel, n_experts=n_experts)
    return pl.pallas_call(
        kern,
        grid=(rows // tm,),
        in_specs=[
            pl.BlockSpec((tm, d), lambda i: (i, 0)),
            pl.BlockSpec((1, d), lambda i: (0, 0)),
            pl.BlockSpec((d, 2 * LANES), lambda i: (0, 0)),
            pl.BlockSpec((1, LANES), lambda i: (0, 0)),
        ],
        out_specs=[
            pl.BlockSpec(_row_tiled(tm, d // 2), lambda i: (i, 0, 0, 0)),
            pl.BlockSpec((tm, LANES), lambda i: (i, 0)),
            pl.BlockSpec((INFO_ROWS, tm), lambda i: (0, i)),
            pl.BlockSpec((1, LANES), lambda i: (0, 0)),
        ],
        out_shape=[
            jax.ShapeDtypeStruct(_row_tiled(rows, d // 2), jnp.uint32),
            jax.ShapeDtypeStruct((rows, LANES), jnp.int32),
            jax.ShapeDtypeStruct((INFO_ROWS, rows), jnp.int32),
            jax.ShapeDtypeStruct((1, LANES), jnp.int32),
        ],
        scratch_shapes=[pltpu.VMEM((1, LANES), f32), pltpu.VMEM((2 * tm, LANES), f32)],
        compiler_params=_cparams(("arbitrary",)),
        name="moe_router",
    )(h, g, w_r, b_r)


def _cast_weight_once(w_ref, wb_ref):
    @pl.when(pl.program_id(0) == 0)
    def _():
        wb_ref[...] = w_ref[...].astype(bf16)


def _matmul_residual_kernel(a_ref, w_ref, r_ref, o_ref, wb_ref):
    _cast_weight_once(w_ref, wb_ref)
    o_ref[...] = r_ref[...] + jnp.dot(a_ref[...], wb_ref[...], preferred_element_type=f32)


def _matmul_residual_split_kernel(a_ref, at_ref, w_ref, r_ref, rt_ref, o_ref, wb_ref, *, n_main):
    _cast_weight_once(w_ref, wb_ref)

    @pl.when(pl.program_id(0) < n_main)
    def _():
        o_ref[...] = r_ref[...] + jnp.dot(a_ref[...], wb_ref[...], preferred_element_type=f32)

    @pl.when(pl.program_id(0) >= n_main)
    def _():
        o_ref[...] = rt_ref[...] + jnp.dot(at_ref[...], wb_ref[...], preferred_element_type=f32)


def _matmul_residual(a, w, r, *, tm):
    rows, k = a.shape
    n = w.shape[1]
    return pl.pallas_call(
        _matmul_residual_kernel,
        grid=(rows // tm,),
        in_specs=[pl.BlockSpec((tm, k), lambda i: (i, 0)),
                  pl.BlockSpec((k, n), lambda i: (0, 0), pipeline_mode=pl.Buffered(1)),
                  pl.BlockSpec((tm, n), lambda i: (i, 0))],
        out_specs=pl.BlockSpec((tm, n), lambda i: (i, 0)),
        out_shape=jax.ShapeDtypeStruct((rows, n), f32),
        scratch_shapes=[pltpu.VMEM((k, n), bf16)],
        compiler_params=_cparams(("arbitrary",)),
        name="out_proj_residual",
    )(a, w, r)


def _matmul_residual_split(a, a_tail, w, r, r_tail, *, tm):
    rows, k = a.shape
    n = w.shape[1]
    n_main = rows // tm
    assert rows % tm == 0 and a_tail.shape[0] == tm and r.shape[0] == rows and r_tail.shape[0] == tm
    kern = functools.partial(_matmul_residual_split_kernel, n_main=n_main)

    def main_blk(i):
        return (jnp.minimum(i, n_main - 1), 0)

    return pl.pallas_call(
        kern,
        grid=(n_main + 1,),
        in_specs=[pl.BlockSpec((tm, k), main_blk),
                  pl.BlockSpec((tm, k), lambda i: (0, 0)),
                  pl.BlockSpec((k, n), lambda i: (0, 0), pipeline_mode=pl.Buffered(1)),
                  pl.BlockSpec((tm, n), main_blk),
                  pl.BlockSpec((tm, n), lambda i: (0, 0))],
        out_specs=pl.BlockSpec((tm, n), lambda i: (i, 0)),
        out_shape=jax.ShapeDtypeStruct((rows + tm, n), f32),
        scratch_shapes=[pltpu.VMEM((k, n), bf16)],
        compiler_params=_cparams(("arbitrary",)),
        name="out_proj_residual",
    )(a, a_tail, w, r, r_tail)


def _pad_fill_copies(fs_ref, pad_ref, zbuf, xs_ref, sem, e):
    fs = fs_ref[e]
    pad = pad_ref[e]
    head = pad & (SUBLANES - 1)
    out = []
    for k in range(SUBLANES - 1):
        out.append((k < head,
                    pltpu.make_async_copy(zbuf.at[0, :, pl.ds(0, 1), :], _row_of(xs_ref, fs + k), sem)))
    cur = lax.shift_right_logical(fs + head, LOG2_SUBLANES)
    for b in reversed(range(SUBLANES.bit_length() - 1, MOE_TILE.bit_length() - 1)):
        size = 1 << b
        groups = size // SUBLANES
        out.append(((pad & size) != 0,
                    pltpu.make_async_copy(zbuf.at[pl.ds(0, groups)], xs_ref.at[pl.ds(cur, groups)], sem)))
        cur = cur + lax.shift_right_logical(pad & size, LOG2_SUBLANES)
    return out


def _unused_tile_copies(zbuf, xs_ref, sem, t):
    groups = zbuf.shape[0]
    per_tile = MOE_TILE // SUBLANES
    return [pltpu.make_async_copy(zbuf, xs_ref.at[pl.ds(t * per_tile + k * groups, groups)], sem)
            for k in range(per_tile // groups)]


def _dispatch_kernel(p0_ref, p1_ref, fs_ref, pad_ref, nu_ref, x_ref, xs_ref, zbuf, sem, zsem, *,
                     n_steps, n_experts, max_tiles):
    i = pl.program_id(0)
    tm = x_ref.shape[0] * SUBLANES
    base = i * tm

    @pl.when(i == 0)
    def _():
        zbuf[...] = jnp.zeros(zbuf.shape, zbuf.dtype)

        def fill(e, c):
            for needed, copy in _pad_fill_copies(fs_ref, pad_ref, zbuf, xs_ref, zsem, e):
                @pl.when(needed)
                def _():
                    copy.start()
            return c

        lax.fori_loop(0, n_experts, fill, 0)

        def fill_unused(t, c):
            for copy in _unused_tile_copies(zbuf, xs_ref, zsem, t):
                copy.start()
            return c

        lax.fori_loop(nu_ref[0], max_tiles, fill_unused, 0)

    def issue(g, c):
        for k in range(SUBLANES):
            src = x_ref.at[g, :, pl.ds(k, 1), :]
            r = base + g * SUBLANES + k
            pltpu.make_async_copy(src, _row_of(xs_ref, p0_ref[r]), sem).start(priority=0)
            pltpu.make_async_copy(src, _row_of(xs_ref, p1_ref[r]), sem).start(priority=1)
        return c

    lax.fori_loop(0, tm // SUBLANES, issue, 0)
    for _ in range(TOP_K):
        pltpu.make_async_copy(x_ref, xs_ref.at[pl.ds(0, tm // SUBLANES)], sem).wait()

    @pl.when(i == n_steps - 1)
    def _():
        def drain(e, c):
            for needed, copy in _pad_fill_copies(fs_ref, pad_ref, zbuf, xs_ref, zsem, e):
                @pl.when(needed)
                def _():
                    copy.wait()
            return c

        lax.fori_loop(0, n_experts, drain, 0)

        def drain_unused(t, c):
            for copy in _unused_tile_copies(zbuf, xs_ref, zsem, t):
                copy.wait()
            return c

        lax.fori_loop(nu_ref[0], max_tiles, drain_unused, 0)


def _dispatch(pos0, pos1, fill_start, pad, n_used, xp, *, sorted_rows, tm):
    width = xp.shape[1] * LANES
    n_steps = (xp.shape[0] * SUBLANES) // tm
    kern = functools.partial(_dispatch_kernel, n_steps=n_steps, n_experts=pad.shape[0],
                             max_tiles=sorted_rows // MOE_TILE)
    return pl.pallas_call(
        kern,
        grid_spec=pltpu.PrefetchScalarGridSpec(
            num_scalar_prefetch=5,
            grid=(n_steps,),
            in_specs=[pl.BlockSpec(_row_tiled(tm, width), lambda i, *_: (i, 0, 0, 0))],
            out_specs=pl.BlockSpec(memory_space=pl.ANY),
            scratch_shapes=[pltpu.VMEM(_row_tiled(MOE_TILE // 2, width), jnp.uint32),
                            pltpu.SemaphoreType.DMA(()), pltpu.SemaphoreType.DMA(())],
        ),
        out_shape=jax.ShapeDtypeStruct(_row_tiled(sorted_rows, width), jnp.uint32),
        compiler_params=pltpu.CompilerParams(dimension_semantics=("arbitrary",),
                                             vmem_limit_bytes=VMEM_LIMIT, has_side_effects=True),
        name="moe_dispatch",
    )(pos0, pos1, fill_start, pad, n_used, xp)


def _experts_kernel(te_ref, tf_ref, nx_ref, nu_ref, xs_ref, wg_hbm, wu_hbm, wd_hbm, y_ref,
                    wg_f, wu_f, wd_f, wg_b, wu_b, wd_b, z_ref, slot_ref, sems, *, layer):
    t = pl.program_id(0)

    def weight_copies(e, slot):
        return [pltpu.make_async_copy(hbm.at[layer, e], buf.at[slot], sems.at[slot, k])
                for k, (hbm, buf) in enumerate(((wg_hbm, wg_f), (wu_hbm, wu_f), (wd_hbm, wd_f)))]

    @pl.when(t == 0)
    def _():
        slot_ref[0] = 0
        for c in weight_copies(te_ref[0], 0):
            c.start()

    @pl.when(tf_ref[t] == 1)
    def _():
        slot = slot_ref[0]
        for c in weight_copies(te_ref[t], slot):
            c.wait()

        @pl.when(nx_ref[t] >= 0)
        def _():
            for c in weight_copies(nx_ref[t], 1 - slot):
                c.start()

        wg_b[...] = wg_f[slot].astype(bf16)
        wu_b[...] = wu_f[slot].astype(bf16)
        wd_b[...] = wd_f[slot].astype(bf16)
        slot_ref[0] = 1 - slot

    @pl.when(t < nu_ref[0])
    def _():
        x = _unpack_rows(xs_ref, z_ref)
        gate = jnp.dot(x, wg_b[...], preferred_element_type=f32)
        up = jnp.dot(x, wu_b[...], preferred_element_type=f32)
        hidden = (gate * (1.0 / (1.0 + jnp.exp(-gate))) * up).astype(bf16)
        _pack_rows(jnp.dot(hidden, wd_b[...], preferred_element_type=f32), y_ref, z_ref)

    @pl.when(t >= nu_ref[0])
    def _():
        y_ref[...] = jnp.zeros(y_ref.shape, y_ref.dtype)


def _experts(tile_expert, tile_first, next_expert, n_used, xs, w_gate, w_up, w_down, *, layer,
             max_tiles):
    tm = MOE_TILE
    _, _, d, f = w_gate.shape
    kern = functools.partial(_experts_kernel, layer=layer)
    hbm = pl.BlockSpec(memory_space=pl.ANY)
    return pl.pallas_call(
        kern,
        grid_spec=pltpu.PrefetchScalarGridSpec(
            num_scalar_prefetch=4,
            grid=(max_tiles,),
            in_specs=[pl.BlockSpec(_row_tiled(tm, d // 2),
                                   lambda t, te, tf, nx, nu: (jnp.clip(nu[0] - 1, 0, t), 0, 0, 0)),
                      hbm, hbm, hbm],
            out_specs=pl.BlockSpec(_row_tiled(tm, d // 2), lambda t, *_: (t, 0, 0, 0)),
            scratch_shapes=[pltpu.VMEM((2, d, f), f32), pltpu.VMEM((2, d, f), f32),
                            pltpu.VMEM((2, f, d), f32),
                            pltpu.VMEM((d, f), bf16), pltpu.VMEM((d, f), bf16),
                            pltpu.VMEM((f, d), bf16), pltpu.VMEM((2 * tm, LANES), f32),
                            pltpu.SMEM((1,), jnp.int32), pltpu.SemaphoreType.DMA((2, 3))],
        ),
        out_shape=jax.ShapeDtypeStruct(_row_tiled(max_tiles * tm, d // 2), jnp.uint32),
        compiler_params=_cparams(("arbitrary",)),
        name="moe_experts",
    )(tile_expert, tile_first, next_expert, n_used, xs, w_gate, w_up, w_down)


def _combine_kernel(p0_ref, p1_ref, h_ref, info_ref, gf_ref, ys_ref, o_ref, buf, z_ref, sems, *,
                    n_steps, final_norm):
    i = pl.program_id(0)
    tm = h_ref.shape[0]

    def issue(step, slot):
        base = step * tm

        def body(g, c):
            for k in range(SUBLANES):
                r = base + g * SUBLANES + k
                pltpu.make_async_copy(_row_of(ys_ref, p0_ref[r]),
                                      buf.at[slot, 0, g, :, pl.ds(k, 1), :],
                                      sems.at[slot]).start(priority=0)
                pltpu.make_async_copy(_row_of(ys_ref, p1_ref[r]),
                                      buf.at[slot, 1, g, :, pl.ds(k, 1), :],
                                      sems.at[slot]).start(priority=1)
            return c

        lax.fori_loop(0, tm // SUBLANES, body, 0)

    @pl.when(i == 0)
    def _():
        issue(0, 0)

    @pl.when(i + 1 < n_steps)
    def _():
        issue(i + 1, lax.rem(i + 1, 2))

    slot = lax.rem(i, 2)
    for k in range(TOP_K):
        pltpu.make_async_copy(ys_ref.at[pl.ds(0, tm // SUBLANES)], buf.at[slot, k],
                              sems.at[slot]).wait()

    info = info_ref[...]
    lane = lax.broadcasted_iota(jnp.int32, info.shape, 1)
    gates = pltpu.bitcast(info, f32)
    g1 = jnp.sum(jnp.where(lane == 4, gates, 0.0), axis=-1, keepdims=True)
    g2 = jnp.sum(jnp.where(lane == 5, gates, 0.0), axis=-1, keepdims=True)
    lo, hi = [], []
    for c in range(buf.shape[3]):
        for k in range(TOP_K):
            words = buf[slot, k, :, c, :, :].reshape(tm, LANES)
            z_ref[k] = pltpu.bitcast(words, bf16).astype(f32)
        lo.append(g1 * z_ref[0, pl.ds(0, tm, stride=2), :] + g2 * z_ref[1, pl.ds(0, tm, stride=2), :])
        hi.append(g1 * z_ref[0, pl.ds(1, tm, stride=2), :] + g2 * z_ref[1, pl.ds(1, tm, stride=2), :])
    out = h_ref[...] + jnp.concatenate(lo + hi, axis=1)
    if final_norm:
        out = _rms_scale(out) * gf_ref[...]
    o_ref[...] = out


def _combine(pos0, pos1, h, info, gf, ys, *, out_rows, tm, final_norm):
    d = h.shape[1]
    n_steps = out_rows // tm
    kern = functools.partial(_combine_kernel, n_steps=n_steps, final_norm=final_norm)
    return pl.pallas_call(
        kern,
        grid_spec=pltpu.PrefetchScalarGridSpec(
            num_scalar_prefetch=2,
            grid=(n_steps,),
            in_specs=[
                pl.BlockSpec((tm, d), lambda i, p0, p1: (i, 0)),
                pl.BlockSpec((tm, LANES), lambda i, p0, p1: (i, 0)),
                pl.BlockSpec((1, d), lambda i, p0, p1: (0, 0)),
                pl.BlockSpec(memory_space=pl.ANY),
            ],
            out_specs=pl.BlockSpec((tm, d), lambda i, p0, p1: (i, 0)),
            scratch_shapes=[pltpu.VMEM((2, TOP_K) + _row_tiled(tm, d // 2), jnp.uint32),
                            pltpu.VMEM((TOP_K, 2 * tm, LANES), f32),
                            pltpu.SemaphoreType.DMA((2,))],
        ),
        out_shape=jax.ShapeDtypeStruct((out_rows, d), f32),
        compiler_params=_cparams(("arbitrary",)),
        name="moe_combine",
    )(pos0, pos1, h, info, gf, ys)


def _table_lookup(table, idx):
    k = lax.broadcasted_iota(jnp.int32, (table.shape[0], idx.shape[0]), 0)
    return jnp.sum(jnp.where(k == idx[None, :], table[:, None], 0), axis=0)


def _router_params(wg_r, bg_r, we_r, be_r):
    d, n_experts = we_r.shape
    n_fill = LANES - N_GROUPS - n_experts
    w_r = jnp.concatenate([wg_r, we_r, jnp.zeros((d, n_fill), f32)], axis=1)
    b_r = jnp.concatenate([bg_r, be_r, jnp.zeros((n_fill,), f32)]).reshape(1, LANES)
    w_hi = w_r.astype(bf16)
    return jnp.concatenate([w_hi, (w_r - w_hi.astype(f32)).astype(bf16)], axis=1), b_r


def _mixer_out_and_moe(a, a_tail, w_out, r, r_tail, g_ffn, wg_r, bg_r, we_r, be_r, w_gate, w_up,
                       w_down, gf, *, layer, out_rows, final_norm):
    d = w_out.shape[1]
    n_experts = we_r.shape[1]
    w_hl, b_r = _router_params(wg_r, bg_r, we_r, be_r)
    if a_tail is None:
        h = _matmul_residual(a, w_out, r, tm=TAIL_ROWS)
    else:
        h = _matmul_residual_split(a, a_tail, w_out, r, r_tail, tm=TAIL_ROWS)
    rows = h.shape[0]
    xp, info, info_t, cnt = _router(h, g_ffn.reshape(1, d), w_hl, b_r, n_experts=n_experts,
                                    tm=TAIL_ROWS)

    tm = MOE_TILE
    max_tiles = (TOP_K * rows) // tm + n_experts
    i32 = jnp.int32
    cnt = cnt[0, N_GROUPS:N_GROUPS + n_experts]
    n_tiles = (cnt + tm - 1) // tm
    tile_end = jnp.cumsum(n_tiles)
    tile_start = tile_end - n_tiles
    n_used = tile_end[-1]
    t_ids = jnp.arange(max_tiles, dtype=i32)[None, :]
    e_ids = jnp.arange(n_experts, dtype=i32)[:, None]
    first_t, end_t = tile_start[:, None], tile_end[:, None]
    member = (first_t <= t_ids) & (t_ids < end_t)
    te = jnp.sum(jnp.where(member, e_ids, 0), axis=0)
    tile_first = jnp.any(member & (first_t == t_ids), axis=0).astype(i32)
    nxt_tile = jnp.sum(jnp.where(member, end_t, 0), axis=0)[None, :]
    is_next = (first_t == nxt_tile) & (n_tiles[:, None] > 0)
    next_expert = jnp.where(jnp.any(is_next, axis=0), jnp.sum(jnp.where(is_next, e_ids, 0), axis=0), -1)

    sorted_rows = max_tiles * tm
    row_off = (tile_start * tm).astype(i32)
    pos0 = jnp.clip(_table_lookup(row_off, info_t[0]) + info_t[2], 0, sorted_rows - 1)
    pos1 = jnp.clip(_table_lookup(row_off, info_t[1]) + info_t[3], 0, sorted_rows - 1)

    n_used = n_used.reshape(1).astype(i32)
    xs = _dispatch(pos0, pos1, (row_off + cnt).astype(i32), (n_tiles * tm - cnt).astype(i32), n_used,
                   xp, sorted_rows=sorted_rows, tm=DISPATCH_ROWS)
    ys = _experts(te.astype(i32), tile_first, next_expert.astype(i32), n_used,
                  xs, w_gate, w_up, w_down, layer=layer, max_tiles=max_tiles)
    return _combine(pos0, pos1, h, info, gf.reshape(1, d), ys, out_rows=out_rows, tm=COMBINE_ROWS,
                    final_norm=final_norm)


def _strict_upper_ones(n):
    r = lax.broadcasted_iota(jnp.int32, (n, n), 0)
    c = lax.broadcasted_iota(jnp.int32, (n, n), 1)
    return (c > r).astype(bf16)


def kernel(x, meta_tokens, norm_mix_g, norm_ffn_g, conv_w_in, conv_w, conv_w_out, attn_w_qkv, attn_w_out, router_group_w, router_group_b, router_expert_w, router_expert_b, moe_w_gate, moe_w_up, moe_w_down, final_norm_g):
    batch, seq_len, d = x.shape
    n_meta = meta_tokens.shape[0]
    n_pad = META_BLOCK - n_meta
    n_tok = batch * seq_len
    assert seq_len % TAIL_ROWS == 0 and d % (2 * N_HEADS) == 0

    x2d = x.reshape(n_tok, d)
    tail = jnp.concatenate([jnp.zeros((n_pad, d), f32), meta_tokens.astype(f32),
                            jnp.zeros((TAIL_ROWS - META_BLOCK, d), f32)], axis=0)

    tc = _largest_divisor(d, CONV_COL_TILES)
    g_mix0 = norm_mix_g[0].reshape(1, d)
    w_in = conv_w_in[0].astype(bf16)
    gated_tail, meta_v = _conv_in(tail, g_mix0, w_in, conv_w[0], jnp.zeros((SUBLANES, d), f32),
                                  seq_len=TAIL_ROWS, tm=TAIL_ROWS, tc=tc)
    gated, _ = _conv_in(x2d, g_mix0, w_in, conv_w[0], meta_v, seq_len=seq_len,
                        tm=_largest_divisor(seq_len, CONV_ROW_TILES), tc=tc)
    h = _mixer_out_and_moe(gated, gated_tail, conv_w_out[0], x2d, tail, norm_ffn_g[0],
                           router_group_w[0], router_group_b[0], router_expert_w[0],
                           router_expert_b[0], moe_w_gate, moe_w_up, moe_w_down, final_norm_g,
                           layer=0, out_rows=n_tok + TAIL_ROWS, final_norm=False)

    qkv = _norm_matmul(h, norm_mix_g[1].reshape(1, d), attn_w_qkv[0].astype(bf16),
                       tm=_largest_divisor(h.shape[0], QKV_ROW_TILES),
                       tn=_largest_divisor(d, QKV_COL_TILES),
                       scaled_cols=d, scale=(d // N_HEADS) ** -0.5)
    o = _attention(qkv, _strict_upper_ones(ATT_BLOCK), batch=batch, seq_len=seq_len, d_model=d,
                   n_pad=n_pad)
    out = _mixer_out_and_moe(o, None, attn_w_out[0], h, None, norm_ffn_g[1],
                             router_group_w[1], router_group_b[1], router_expert_w[1],
                             router_expert_b[1], moe_w_gate, moe_w_up, moe_w_down, final_norm_g,
                             layer=1, out_rows=n_tok, final_norm=True)
    return out.reshape(batch, seq_len, d)
```
